```python
import jax, jax.numpy as jnp
from jax import lax
import numpy as np

D_MODEL = 2048
BATCH = 2
SEQ = 4096
DEPTH = 2
DEC_BATCH = 128
DEC_SEQ = 8
PAST_LEN = 2048
PAGE_SIZE = 128

D_HEAD = 128
POOL_WIDTH = D_MODEL // 4
POOL_GROUPS = 4
POOL_GROUP_DIM = POOL_WIDTH // POOL_GROUPS
POOL_WINDOWS = (2, 4, 8, 16)
POOL_BUF = 15
SB_WIDTH = D_MODEL // 4
SB_HEADS = SB_WIDTH // D_HEAD
MB_WIDTH = D_MODEL // 4
MB_HEADS = MB_WIDTH // D_HEAD
MEM_WIDTH = D_MODEL - POOL_WIDTH - SB_WIDTH - MB_WIDTH
MEM_HEADS = MEM_WIDTH // D_HEAD
N_MEM = 256
MOBA_BLOCK = 256
MOBA_TOPK = 3
Q_BLOCK = 128
D_FF = 4 * D_MODEL
ROPE_THETA = 10000.0
LN_EPS = 1e-5
NEG_INF = -1e30
DEEPNORM_ALPHA = (2 * DEPTH) ** 0.25
DEEPNORM_BETA = (8 * DEPTH) ** -0.25
IN_WIDTH = POOL_WIDTH + 3 * SB_WIDTH + 3 * MB_WIDTH + MEM_WIDTH
MIX_WIDTH = POOL_WIDTH + SB_WIDTH + MB_WIDTH + MEM_WIDTH

kernel_name = "hymba_pool_stickbreak_moba_memxattn_decoder_step"


def layer_norm(x, g, b):
    xf = x.astype(jnp.float32)
    mu = xf.mean(-1, keepdims=True)
    var = jnp.square(xf - mu).mean(-1, keepdims=True)
    y = (xf - mu) * lax.rsqrt(var + LN_EPS)
    return (y * g.astype(jnp.float32) + b.astype(jnp.float32)).astype(x.dtype)


def rope(x, pos):
    half = D_HEAD // 2
    inv = 1.0 / (ROPE_THETA ** (jnp.arange(half, dtype=jnp.float32) / half))
    ang = pos.astype(jnp.float32)[:, None] * inv[None, :]
    cos = jnp.cos(ang)[None, :, None, :]
    sin = jnp.sin(ang)[None, :, None, :]
    xf = x.astype(jnp.float32)
    x1, x2 = xf[..., :half], xf[..., half:]
    return jnp.concatenate([x1 * cos - x2 * sin, x2 * cos + x1 * sin], -1).astype(x.dtype)


def project_in(x, w_in):
    B, T, _ = x.shape
    h = x @ w_in
    widths = (POOL_WIDTH, SB_WIDTH, SB_WIDTH, SB_WIDTH, MB_WIDTH, MB_WIDTH, MB_WIDTH, MEM_WIDTH)
    cuts = np.cumsum(widths)[:-1].tolist()
    u, q_sb, k_sb, v_sb, q_mb, k_mb, v_mb, q_mem = jnp.split(h, cuts, axis=-1)
    hd = lambda a: a.reshape(B, T, -1, D_HEAD)
    return u, hd(q_sb), hd(k_sb), hd(v_sb), hd(q_mb), hd(k_mb), hd(v_mb), hd(q_mem)


def pool_mix(u_ext, n_prev, pos_new, pool_w, pool_scale):
    B, L, _ = u_ext.shape
    T = L - n_prev
    uf = u_ext.astype(jnp.float32).reshape(B, L, POOL_GROUPS, POOL_GROUP_DIM)
    csum = jnp.concatenate([jnp.zeros((B, 1, POOL_GROUPS, POOL_GROUP_DIM), jnp.float32),
                            jnp.cumsum(uf, axis=1)], axis=1)
    win = jnp.asarray(POOL_WINDOWS, jnp.int32)
    end = n_prev + 1 + jnp.arange(T, dtype=jnp.int32)
    start = jnp.maximum(end[:, None] - win[None, :], 0)
    count = jnp.minimum(win[None, :], pos_new[:, None] + 1).astype(jnp.float32)
    grp = jnp.arange(POOL_GROUPS)
    window_sum = csum[:, end] - csum[:, start, grp]
    mixed = window_sum / count[None, :, :, None] - uf[:, n_prev:]
    y = jnp.einsum('btgc,gcd->btgd', mixed.astype(u_ext.dtype), pool_w)
    return y.reshape(B, T, POOL_WIDTH) * pool_scale


def sb_attend(q, k, v, q_pos, k_pos):
    z = jnp.einsum('bqhd,bkhd->bhqk', q, k).astype(jnp.float32) * (D_HEAD ** -0.5)
    mask = k_pos[None, :] < q_pos[:, None]
    log_fail = jnp.where(mask, jax.nn.log_sigmoid(-z), 0.0)
    between = lax.cumsum(log_fail, axis=3, reverse=True) - log_fail
    w = jnp.where(mask, jnp.exp(jax.nn.log_sigmoid(z) + between), 0.0)
    return jnp.einsum('bhqk,bkhd->bqhd', w.astype(v.dtype), v)


def moba_key_blocks(k, v):
    B, L, H, d = k.shape
    pad = (-L) % MOBA_BLOCK
    if pad:
        z = jnp.zeros((B, pad, H, d), k.dtype)
        k = jnp.concatenate([k, z], 1)
        v = jnp.concatenate([v, z.astype(v.dtype)], 1)
    nb = (L + pad) // MOBA_BLOCK
    kb = k.reshape(B, nb, MOBA_BLOCK, H, d)
    vb = v.reshape(B, nb, MOBA_BLOCK, H, d)
    return kb, vb, kb.astype(jnp.float32).mean(2)


def moba_attend(q, kb, vb, kmean, q_pos):
    B, nb = kb.shape[0], kb.shape[1]
    H = q.shape[2]
    n_sel = min(MOBA_TOPK, nb)
    own = q_pos // MOBA_BLOCK
    gate = jnp.einsum('bqhd,bnhd->bhqn', q.astype(jnp.float32), kmean)
    fully_past = jnp.arange(nb)[None, :] < own[:, None]
    gate = jnp.where(fully_past, gate, NEG_INF)
    _, sel = lax.top_k(gate, n_sel)
    own_b = jnp.broadcast_to(own[None, None, :, None], sel.shape[:3] + (1,)).astype(sel.dtype)
    blocks = jnp.concatenate([sel, own_b], axis=-1)
    bi = jnp.arange(B)[:, None, None, None]
    hi = jnp.arange(H)[None, :, None, None]
    kg = kb[bi, blocks, :, hi]
    vg = vb[bi, blocks, :, hi]
    s = jnp.einsum('bqhd,bhqnkd->bhqnk', q, kg).astype(jnp.float32) * (D_HEAD ** -0.5)
    kpos = blocks[..., None] * MOBA_BLOCK + jnp.arange(MOBA_BLOCK)
    is_sel = (jnp.arange(n_sel + 1) < n_sel)[:, None]
    valid = jnp.where(is_sel, (blocks < own[:, None])[..., None], kpos <= q_pos[:, None, None])
    s = jnp.where(valid, s, NEG_INF)
    p = jax.nn.softmax(s.reshape(s.shape[:3] + (-1,)), axis=-1).reshape(s.shape)
    return jnp.einsum('bhqnk,bhqnkd->bqhd', p.astype(vg.dtype), vg)


def mem_kv(mem, w_mem_k, w_mem_v):
    B, N, _ = mem.shape
    return ((mem @ w_mem_k).reshape(B, N, MEM_HEADS, D_HEAD),
            (mem @ w_mem_v).reshape(B, N, MEM_HEADS, D_HEAD))


def mem_attend(q, mk, mv):
    s = jnp.einsum('bthd,bnhd->bhtn', q, mk).astype(jnp.float32) * (D_HEAD ** -0.5)
    p = jax.nn.softmax(s, axis=-1).astype(mv.dtype)
    return jnp.einsum('bhtn,bnhd->bthd', p, mv)


def over_query_blocks(fn, q, q_pos):
    B, S, H, d = q.shape
    nc = S // Q_BLOCK
    qc = jnp.moveaxis(q.reshape(B, nc, Q_BLOCK, H, d), 1, 0)
    pc = q_pos.reshape(nc, Q_BLOCK)
    out = lax.map(lambda a: fn(a[0], a[1]), (qc, pc))
    return jnp.moveaxis(out, 0, 1).reshape(B, S, H, d)


def finish_layer(x, parts, w_out, ln1_g, ln1_b, w_up, w_down, ln2_g, ln2_b):
    B, T, _ = x.shape
    mix = jnp.concatenate([p.reshape(B, T, -1) for p in parts], -1)
    x = layer_norm(DEEPNORM_ALPHA * x + mix @ w_out, ln1_g, ln1_b)
    hid = jnp.square(jax.nn.relu(x @ w_up))
    return layer_norm(DEEPNORM_ALPHA * x + hid @ w_down, ln2_g, ln2_b)


def prompt_layer(x, mem, w_in, w_mem_k, w_mem_v, pool_w, pool_scale, w_out,
                 ln1_g, ln1_b, w_up, w_down, ln2_g, ln2_b):
    B, S, _ = x.shape
    pos = jnp.arange(S, dtype=jnp.int32)
    u, q_sb, k_sb, v_sb, q_mb, k_mb, v_mb, q_mem = project_in(x, w_in)
    y_pool = pool_mix(u, 0, pos, pool_w, pool_scale)
    y_sb = over_query_blocks(lambda qc, pc: sb_attend(qc, k_sb, v_sb, pc, pos), q_sb, pos)
    q_mb = rope(q_mb, pos)
    k_mb = rope(k_mb, pos)
    kb, vb, km = moba_key_blocks(k_mb, v_mb)
    y_mb = over_query_blocks(lambda qc, pc: moba_attend(qc, kb, vb, km, pc), q_mb, pos)
    mk, mv = mem_kv(mem, w_mem_k, w_mem_v)
    y_mem = mem_attend(q_mem, mk, mv)
    y = finish_layer(x, [y_pool, y_sb, y_mb, y_mem], w_out, ln1_g, ln1_b, w_up, w_down, ln2_g, ln2_b)
    return y, (k_sb, v_sb, k_mb, v_mb, mk, mv, u[:, -POOL_BUF:])


def paged_rows(pool, layer, page_table):
    rows = pool[layer, page_table]
    B, NP, P, H, d = rows.shape
    return rows.reshape(B, NP * P, H, d)


def sample_layer(layer, x, cache_sb_k, cache_sb_v, cache_mb_k, cache_mb_v, mem_k, mem_v,
                 pool_buf, page_table, w_in, pool_w, pool_scale, w_out,
                 ln1_g, ln1_b, w_up, w_down, ln2_g, ln2_b):
    B, T, _ = x.shape
    past = page_table.shape[1] * PAGE_SIZE
    pos = past + jnp.arange(T, dtype=jnp.int32)
    k_pos = jnp.arange(past + T, dtype=jnp.int32)
    u, q_sb, k_sb, v_sb, q_mb, k_mb, v_mb, q_mem = project_in(x, w_in)
    u_ext = jnp.concatenate([pool_buf.astype(u.dtype), u], 1)
    y_pool = pool_mix(u_ext, POOL_BUF, pos, pool_w, pool_scale)
    k_all = jnp.concatenate([paged_rows(cache_sb_k, layer, page_table).astype(k_sb.dtype), k_sb], 1)
    v_all = jnp.concatenate([paged_rows(cache_sb_v, layer, page_table).astype(v_sb.dtype), v_sb], 1)
    y_sb = sb_attend(q_sb, k_all, v_all, pos, k_pos)
    q_mb = rope(q_mb, pos)
    k_mb = rope(k_mb, pos)
    kb, vb, km = moba_key_blocks(
        jnp.concatenate([paged_rows(cache_mb_k, layer, page_table).astype(k_mb.dtype), k_mb], 1),
        jnp.concatenate([paged_rows(cache_mb_v, layer, page_table).astype(v_mb.dtype), v_mb], 1))
    y_mb = moba_attend(q_mb, kb, vb, km, pos)
    y_mem = mem_attend(q_mem, mem_k.astype(q_mem.dtype), mem_v.astype(q_mem.dtype))
    y = finish_layer(x, [y_pool, y_sb, y_mb, y_mem], w_out, ln1_g, ln1_b, w_up, w_down, ln2_g, ln2_b)
    return y, (k_sb, v_sb, k_mb, v_mb, u_ext[:, -POOL_BUF:])


def setup_inputs(seed: int = 0) -> dict:
    key = jax.random.key(seed)
    ks = jax.random.split(key, 24)
    n_pages = PAST_LEN // PAGE_SIZE
    n_used = DEC_BATCH * n_pages
    n_phys = n_used + n_used // 4
    nrm = lambda k, shape, s=1.0: jax.random.normal(k, shape, jnp.float32) * s
    page_table = jax.random.permutation(ks[0], n_phys)[:n_used].reshape(DEC_BATCH, n_pages).astype(jnp.int32)
    kv_pool = (DEPTH, n_phys, PAGE_SIZE, SB_HEADS, D_HEAD)
    mb_pool = (DEPTH, n_phys, PAGE_SIZE, MB_HEADS, D_HEAD)
    mem_c = (DEPTH, DEC_BATCH, N_MEM, MEM_HEADS, D_HEAD)
    return dict(
        x_prompt=nrm(ks[1], (BATCH, SEQ, D_MODEL)),
        x_sample=nrm(ks[2], (DEC_BATCH, DEC_SEQ, D_MODEL)),
        cache_sb_k=nrm(ks[3], kv_pool),
        cache_sb_v=nrm(ks[4], kv_pool),
        cache_mb_k=nrm(ks[5], mb_pool),
        cache_mb_v=nrm(ks[6], mb_pool),
        cache_mem_k=nrm(ks[7], mem_c),
        cache_mem_v=nrm(ks[8], mem_c),
        state_pool=nrm(ks[9], (DEPTH, DEC_BATCH, POOL_BUF, POOL_WIDTH)),
        page_table=page_table,
        mem_prompt=nrm(ks[10], (BATCH, N_MEM, D_MODEL)),
        w_in=nrm(ks[11], (DEPTH, D_MODEL, IN_WIDTH), D_MODEL ** -0.5),
        w_mem_k=nrm(ks[12], (DEPTH, D_MODEL, MEM_WIDTH), D_MODEL ** -0.5),
        w_mem_v=nrm(ks[13], (DEPTH, D_MODEL, MEM_WIDTH), D_MODEL ** -0.5),
        pool_w=nrm(ks[14], (DEPTH, POOL_GROUPS, POOL_GROUP_DIM, POOL_GROUP_DIM), POOL_GROUP_DIM ** -0.5),
        pool_scale=1.0 + nrm(ks[15], (DEPTH, POOL_WIDTH), 0.02),
        w_out=nrm(ks[16], (DEPTH, MIX_WIDTH, D_MODEL), MIX_WIDTH ** -0.5 * DEEPNORM_BETA),
        ln1_g=1.0 + nrm(ks[17], (DEPTH, D_MODEL), 0.02),
        ln1_b=nrm(ks[18], (DEPTH, D_MODEL), 0.02),
        w_up=nrm(ks[19], (DEPTH, D_MODEL, D_FF), D_MODEL ** -0.5),
        w_down=nrm(ks[20], (DEPTH, D_FF, D_MODEL), D_FF ** -0.5 * DEEPNORM_BETA),
        ln2_g=1.0 + nrm(ks[21], (DEPTH, D_MODEL), 0.02),
        ln2_b=nrm(ks[22], (DEPTH, D_MODEL), 0.02),
    )


def _stack(states, i):
    return jnp.stack([st[i] for st in states], axis=0)


def reference(x_prompt, x_sample, cache_sb_k, cache_sb_v, cache_mb_k, cache_mb_v,
              cache_mem_k, cache_mem_v, state_pool, page_table, mem_prompt,
              w_in, w_mem_k, w_mem_v, pool_w, pool_scale, w_out, ln1_g, ln1_b,
              w_up, w_down, ln2_g, ln2_b):
    yp, ys = x_prompt, x_sample
    ps, ss = [], []
    for l in range(DEPTH):
        yp, st = prompt_layer(yp, mem_prompt, w_in[l], w_mem_k[l], w_mem_v[l], pool_w[l],
                              pool_scale[l], w_out[l], ln1_g[l], ln1_b[l], w_up[l], w_down[l],
                              ln2_g[l], ln2_b[l])
        ps.append(st)
        ys, st = sample_layer(l, ys, cache_sb_k, cache_sb_v, cache_mb_k, cache_mb_v,
                              cache_mem_k[l], cache_mem_v[l], state_pool[l], page_table,
                              w_in[l], pool_w[l], pool_scale[l], w_out[l], ln1_g[l], ln1_b[l],
                              w_up[l], w_down[l], ln2_g[l], ln2_b[l])
        ss.append(st)
    return (yp, ys,
            _stack(ps, 0), _stack(ps, 1), _stack(ps, 2), _stack(ps, 3),
            _stack(ps, 4), _stack(ps, 5), _stack(ps, 6),
            _stack(ss, 0), _stack(ss, 1), _stack(ss, 2), _stack(ss, 3), _stack(ss, 4))
```

```python
import functools

import numpy as np
import jax
import jax.numpy as jnp
from jax import lax
from jax.experimental import pallas as pl
from jax.experimental.pallas import tpu as pltpu

F32 = jnp.float32
BF16 = jnp.bfloat16

D_HEAD = 128
N_HEADS = 4
GROUP_W = N_HEADS * D_HEAD
N_GROUPS_IN = 8
ROPE_GROUPS = (4, 5)
POOL_WINDOWS = (2, 4, 8, 16)
POOL_BUF = 15
POOL_HALO = 16
MOBA_BLOCK = 256
MOBA_TOPK = 3
PAGE_SIZE = 128
ROPE_THETA = 10000.0
LN_EPS = 1e-5
NEG_INF = -1e30
ATT_SCALE = D_HEAD ** -0.5

VMEM_LIMIT_BYTES = 56 * 1024 * 1024


def _params(*sem):
    return pltpu.CompilerParams(dimension_semantics=sem, vmem_limit_bytes=VMEM_LIMIT_BYTES)


def _dot(a, b):
    return jnp.dot(a, b, preferred_element_type=F32)


def _dot_nt(a, b):
    return lax.dot_general(a, b, (((1,), (1,)), ((), ())), preferred_element_type=F32)


def _split_hi_lo(x):
    hi = x.astype(BF16)
    lo = (x - hi.astype(F32)).astype(BF16)
    return hi, lo


def _softplus(z):
    return jnp.maximum(z, 0.0) + jnp.log1p(jnp.exp(-jnp.abs(z)))


def _layer_norm(x, g, b):
    mu = jnp.mean(x, axis=-1, keepdims=True)
    xc = x - mu
    var = jnp.mean(xc * xc, axis=-1, keepdims=True)
    return xc * lax.rsqrt(var + LN_EPS) * g + b


def _inproj_body(x_ref, w_ref, cos_ref, sin_ref, *refs):
    outs = refs[:N_GROUPS_IN]
    xb_ref = refs[N_GROUPS_IN]
    j = pl.program_id(1)

    @pl.when(j == 0)
    def _cast():
        xb_ref[...] = x_ref[...].astype(BF16)

    h = _dot(xb_ref[...], w_ref[...])
    for g in range(N_GROUPS_IN):
        @pl.when(j == g)
        def _store(g=g):
            if g in ROPE_GROUPS:
                cos = cos_ref[...]
                sin = sin_ref[...]
                for c in range(N_HEADS):
                    hs = h[:, c * D_HEAD:(c + 1) * D_HEAD]
                    outs[g][:, c * D_HEAD:(c + 1) * D_HEAD] = (
                        hs * cos + pltpu.roll(hs, D_HEAD // 2, 1) * sin)
            else:
                outs[g][...] = h


def _in_projection(x, w_bf16, cos_tab, sin_tab, tm):
    rows, d_model = x.shape
    assert w_bf16.shape == (d_model, N_GROUPS_IN * GROUP_W) and rows % tm == 0
    out_sds = jax.ShapeDtypeStruct((rows, GROUP_W), F32)
    return pl.pallas_call(
        _inproj_body,
        grid=(rows // tm, N_GROUPS_IN),
        in_specs=[
            pl.BlockSpec((tm, d_model), lambda i, j: (i, 0)),
            pl.BlockSpec((d_model, GROUP_W), lambda i, j: (0, j)),
            pl.BlockSpec((tm, D_HEAD), lambda i, j: (i, 0)),
            pl.BlockSpec((tm, D_HEAD), lambda i, j: (i, 0)),
        ],
        out_specs=[pl.BlockSpec((tm, GROUP_W), lambda i, j: (i, 0))] * N_GROUPS_IN,
        out_shape=[out_sds] * N_GROUPS_IN,
        scratch_shapes=[pltpu.VMEM((tm, d_model), BF16)],
        compiler_params=_params("arbitrary", "arbitrary"),
        name="in_projection",
    )(x, w_bf16, cos_tab, sin_tab)


def _mm_body(x_ref, w_ref, o_ref):
    o_ref[...] = _dot(x_ref[...].astype(BF16), w_ref[...])


def _matmul(x, w_bf16):
    m, k = x.shape
    n = w_bf16.shape[1]
    return pl.pallas_call(
        _mm_body,
        grid=(1,),
        in_specs=[pl.BlockSpec((m, k), lambda i: (0, 0)), pl.BlockSpec((k, n), lambda i: (0, 0))],
        out_specs=pl.BlockSpec((m, n), lambda i: (0, 0)),
        out_shape=jax.ShapeDtypeStruct((m, n), F32),
        compiler_params=_params("arbitrary"),
        name="mem_projection",
    )(x, w_bf16)


def _pool_finish(sums, u_groups, counts, pw_ref, sc_ref):
    ys = []
    for g in range(len(POOL_WINDOWS)):
        mixed = sums[g] / counts[g] - u_groups[g]
        ys.append(_dot(mixed.astype(BF16), pw_ref[g]))
    return jnp.concatenate(ys, axis=1) * sc_ref[...]


def _pool_prompt_body(u_ref, pw_ref, sc_ref, o_ref, ext_ref, *, tt):
    i = pl.program_id(1)
    gd = GROUP_W // len(POOL_WINDOWS)

    @pl.when(i == 0)
    def _reset():
        ext_ref[0:POOL_HALO, :] = jnp.zeros((POOL_HALO, GROUP_W), F32)

    ext_ref[POOL_HALO:POOL_HALO + tt, :] = u_ref[...]
    sums = [None] * len(POOL_WINDOWS)
    for s in range(max(POOL_WINDOWS)):
        g0 = min(g for g, w in enumerate(POOL_WINDOWS) if s < w)
        sh = ext_ref[pl.ds(POOL_HALO - s, tt), pl.ds(g0 * gd, GROUP_W - g0 * gd)]
        for g in range(g0, len(POOL_WINDOWS)):
            piece = sh[:, (g - g0) * gd:(g - g0 + 1) * gd]
            sums[g] = piece if sums[g] is None else sums[g] + piece
    pos = i * tt + lax.broadcasted_iota(jnp.int32, (tt, gd), 0)
    counts = [jnp.minimum(w, pos + 1).astype(F32) for w in POOL_WINDOWS]
    u = u_ref[...]
    u_groups = [u[:, g * gd:(g + 1) * gd] for g in range(len(POOL_WINDOWS))]
    o_ref[...] = _pool_finish(sums, u_groups, counts, pw_ref, sc_ref)
    ext_ref[0:POOL_HALO, :] = ext_ref[tt:tt + POOL_HALO, :]


def _pool_prompt(u_all, pw_bf16, scale, batch, seq, tt):
    nt = seq // tt
    return pl.pallas_call(
        functools.partial(_pool_prompt_body, tt=tt),
        grid=(batch, nt),
        in_specs=[
            pl.BlockSpec((tt, GROUP_W), lambda b, i: (b * nt + i, 0)),
            pl.BlockSpec(pw_bf16.shape, lambda b, i: (0, 0, 0)),
            pl.BlockSpec((1, GROUP_W), lambda b, i: (0, 0)),
        ],
        out_specs=pl.BlockSpec((tt, GROUP_W), lambda b, i: (b * nt + i, 0)),
        out_shape=jax.ShapeDtypeStruct((batch * seq, GROUP_W), F32),
        scratch_shapes=[pltpu.VMEM((POOL_HALO + tt, GROUP_W), F32)],
        compiler_params=_params("arbitrary", "arbitrary"),
        name="pool_prompt",
    )(u_all, pw_bf16, scale)


def _pool_sample_body(ue_ref, pw_ref, sc_ref, o_ref, *, past, t_new):
    nseq = ue_ref.shape[0]
    gd = GROUP_W // len(POOL_WINDOWS)
    sums = [None] * len(POOL_WINDOWS)
    for s in range(max(POOL_WINDOWS)):
        g0 = min(g for g, w in enumerate(POOL_WINDOWS) if s < w)
        sh = ue_ref[:, pl.ds(POOL_HALO - s, t_new), pl.ds(g0 * gd, GROUP_W - g0 * gd)]
        for g in range(g0, len(POOL_WINDOWS)):
            piece = sh[:, :, (g - g0) * gd:(g - g0 + 1) * gd].reshape(nseq * t_new, gd)
            sums[g] = piece if sums[g] is None else sums[g] + piece
    pos = past + lax.broadcasted_iota(jnp.int32, (nseq, t_new, gd), 1).reshape(nseq * t_new, gd)
    counts = [jnp.minimum(w, pos + 1).astype(F32) for w in POOL_WINDOWS]
    u = ue_ref[:, pl.ds(POOL_HALO, t_new), :]
    u_groups = [u[:, :, g * gd:(g + 1) * gd].reshape(nseq * t_new, gd) for g in range(len(POOL_WINDOWS))]
    o_ref[...] = _pool_finish(sums, u_groups, counts, pw_ref, sc_ref)


def _pool_sample(u_ext, pw_bf16, scale, past, t_new):
    nseq = u_ext.shape[0]
    return pl.pallas_call(
        functools.partial(_pool_sample_body, past=past, t_new=t_new),
        grid=(1,),
        in_specs=[
            pl.BlockSpec(u_ext.shape, lambda i: (0, 0, 0)),
            pl.BlockSpec(pw_bf16.shape, lambda i: (0, 0, 0)),
            pl.BlockSpec((1, GROUP_W), lambda i: (0, 0)),
        ],
        out_specs=pl.BlockSpec((nseq * t_new, GROUP_W), lambda i: (0, 0)),
        out_shape=jax.ShapeDtypeStruct((nseq * t_new, GROUP_W), F32),
        compiler_params=_params("arbitrary"),
        name="pool_sample",
    )(u_ext, pw_bf16, scale)


def _suffix_sum_matrix(n):
    row = lax.broadcasted_iota(jnp.int32, (n, n), 0)
    col = lax.broadcasted_iota(jnp.int32, (n, n), 1)
    upper = jnp.where(row > col, 1.0, 0.0).astype(BF16)
    return jnp.concatenate([upper, jnp.ones((n, n), BF16)], axis=1)


def _sb_block(qb, k, v, u2, carry, valid):
    n = k.shape[0]
    z = _dot_nt(qb, k) * ATT_SCALE
    sp = _softplus(z)
    log_fail = -sp
    if valid is not None:
        log_fail = jnp.where(valid, log_fail, 0.0)
    hi, lo = _split_hi_lo(log_fail)
    cst = _dot(hi, u2) + _dot(lo, u2)
    w = jnp.exp(z - sp + cst[:, :n] + carry)
    if valid is not None:
        w = jnp.where(valid, w, 0.0)
    return _dot(w.astype(BF16), v), carry + cst[:, n:]


def _sb_prompt_body(q_ref, k_ref, v_ref, o_ref, kb_ref, vb_ref, acc_ref, car_ref, *, tq):
    qi = pl.program_id(2)

    @pl.when(qi == 0)
    def _cast():
        kb_ref[...] = k_ref[...].astype(BF16)
        vb_ref[...] = v_ref[...].astype(BF16)

    qb = q_ref[...].astype(BF16)
    u2 = _suffix_sum_matrix(tq)
    row = lax.broadcasted_iota(jnp.int32, (tq, tq), 0)
    col = lax.broadcasted_iota(jnp.int32, (tq, tq), 1)

    def block(kb, carry, valid):
        off = pl.multiple_of(kb * tq, tq)
        k = kb_ref[pl.ds(off, tq), :]
        v = vb_ref[pl.ds(off, tq), :]
        return _sb_block(qb, k, v, u2, carry, valid)

    pv, car = block(qi, jnp.zeros((tq, tq), F32), col < row)
    acc_ref[...] = pv
    car_ref[...] = car

    def past(t, c):
        pv, car = block(qi - 1 - t, car_ref[...], None)
        acc_ref[...] += pv
        car_ref[...] = car
        return c

    lax.fori_loop(0, qi, past, 0)
    o_ref[...] = acc_ref[...]


def _sb_prompt(q_all, k_all, v_all, batch, seq, tq):
    nq = seq // tq
    kv_spec = pl.BlockSpec((seq, D_HEAD), lambda b, h, i: (b, h))
    return pl.pallas_call(
        functools.partial(_sb_prompt_body, tq=tq),
        grid=(batch, N_HEADS, nq),
        in_specs=[pl.BlockSpec((tq, D_HEAD), lambda b, h, i: (b * nq + i, h)), kv_spec, kv_spec],
        out_specs=pl.BlockSpec((tq, D_HEAD), lambda b, h, i: (b * nq + i, h)),
        out_shape=jax.ShapeDtypeStruct((batch * seq, GROUP_W), F32),
        scratch_shapes=[pltpu.VMEM((seq, D_HEAD), BF16), pltpu.VMEM((seq, D_HEAD), BF16),
                        pltpu.VMEM((tq, D_HEAD), F32), pltpu.VMEM((tq, tq), F32)],
        compiler_params=_params("arbitrary", "arbitrary", "arbitrary"),
        name="sb_prompt",
    )(q_all, k_all, v_all)


def _head_masks(t_new):
    rows = N_HEADS * t_new
    r = lax.broadcasted_iota(jnp.int32, (rows, GROUP_W), 0)
    c = lax.broadcasted_iota(jnp.int32, (rows, GROUP_W), 1)
    same = None
    for h in range(N_HEADS):
        m = (r >= h * t_new) & (r < (h + 1) * t_new) & (c >= h * D_HEAD) & (c < (h + 1) * D_HEAD)
        same = m if same is None else same | m
    return same


def _stack_heads(q, t_new):
    qrep = jnp.concatenate([q] * N_HEADS, axis=0)
    return jnp.where(_head_masks(t_new), qrep, 0.0)


def _unstack_heads(acc, t_new):
    masked = jnp.where(_head_masks(t_new), acc, 0.0)
    out = masked[0:t_new]
    for h in range(1, N_HEADS):
        out = out + masked[h * t_new:(h + 1) * t_new]
    return out


def _row_time(t_new, width):
    r = lax.broadcasted_iota(jnp.int32, (N_HEADS * t_new, width), 0)
    t = r
    for h in range(1, N_HEADS):
        t = jnp.where(r >= h * t_new, r - h * t_new, t)
    return t


def _pad_rows(x, rows):
    return jnp.concatenate([x, jnp.zeros((rows - x.shape[0], x.shape[1]), x.dtype)], axis=0)


def _sb_sample_body(pt_ref, q_ref, kn_ref, vn_ref, *refs, npages, t_new):
    kp, vp, o_ref = refs[:npages], refs[npages:2 * npages], refs[2 * npages]
    rows = N_HEADS * t_new
    qb = _stack_heads(q_ref[...], t_new).astype(BF16)
    u2 = _suffix_sum_matrix(PAGE_SIZE)
    col = lax.broadcasted_iota(jnp.int32, (rows, PAGE_SIZE), 1)
    valid_new = col < _row_time(t_new, PAGE_SIZE)
    carry = jnp.zeros((rows, PAGE_SIZE), F32)
    kn = _pad_rows(kn_ref[...], PAGE_SIZE).astype(BF16)
    vn = _pad_rows(vn_ref[...], PAGE_SIZE).astype(BF16)
    acc, carry = _sb_block(qb, kn, vn, u2, carry, valid_new)
    for p in range(npages - 1, -1, -1):
        pv, carry = _sb_block(qb, kp[p][...].astype(BF16), vp[p][...].astype(BF16), u2, carry, None)
        acc = acc + pv
    o_ref[...] = _unstack_heads(acc, t_new)


def _paged_specs(layer, npages):
    def spec(p):
        return pl.BlockSpec((None, None, PAGE_SIZE, GROUP_W),
                            lambda b, pt: (layer, pt[b * npages + p], 0, 0))
    return [spec(p) for p in range(npages)]


def _sb_sample(layer, page_table, q_all, k_all, v_all, cache_k, cache_v, row0, nseq, t_new):
    npages = page_table.shape[1]
    blk0 = row0 // t_new
    new_spec = pl.BlockSpec((t_new, GROUP_W), lambda b, pt: (blk0 + b, 0))
    grid_spec = pltpu.PrefetchScalarGridSpec(
        num_scalar_prefetch=1,
        grid=(nseq,),
        in_specs=[new_spec, new_spec, new_spec] + _paged_specs(layer, npages) + _paged_specs(layer, npages),
        out_specs=pl.BlockSpec((t_new, GROUP_W), lambda b, pt: (b, 0)),
    )
    return pl.pallas_call(
        functools.partial(_sb_sample_body, npages=npages, t_new=t_new),
        grid_spec=grid_spec,
        out_shape=jax.ShapeDtypeStruct((nseq * t_new, GROUP_W), F32),
        compiler_params=_params("arbitrary"),
        name="sb_sample",
    )(page_table.reshape(-1), q_all, k_all, v_all, *([cache_k] * npages), *([cache_v] * npages))


def _moba_prompt_body(q_ref, k_ref, v_ref, o_ref, kb_ref, vb_ref, km_ref, sel_ref, *, tq, nb):
    qi = pl.program_id(2)

    @pl.when(qi == 0)
    def _prep():
        kb_ref[...] = k_ref[...].astype(BF16)
        vb_ref[...] = v_ref[...].astype(BF16)
        km_ref[...] = jnp.zeros(km_ref.shape, F32)
        for n in range(nb):
            blk = k_ref[n * MOBA_BLOCK:(n + 1) * MOBA_BLOCK, :]
            km_ref[n:n + 1, :] = jnp.sum(blk, axis=0, keepdims=True) * (1.0 / MOBA_BLOCK)

    q = q_ref[...]
    qb = q.astype(BF16)
    q_hi, q_lo = _split_hi_lo(q)
    km_hi, km_lo = _split_hi_lo(km_ref[...])
    gate = _dot_nt(q_hi, km_hi) + _dot_nt(q_lo, km_hi) + _dot_nt(q_hi, km_lo)
    lane = lax.broadcasted_iota(jnp.int32, gate.shape, 1)
    past_blk = lane < qi
    gate = jnp.where(past_blk, gate, NEG_INF)
    rank = jnp.zeros(gate.shape, jnp.int32)
    for m in range(nb):
        gm = gate[:, m:m + 1]
        beats = (gm > gate) | ((gm == gate) & (m < lane))
        rank = rank + beats.astype(jnp.int32)
    sel_ref[...] = jnp.where(past_blk & (rank < MOBA_TOPK), 1.0, 0.0)

    row = lax.broadcasted_iota(jnp.int32, (tq, tq), 0)
    col = lax.broadcasted_iota(jnp.int32, (tq, tq), 1)

    def scores(n):
        off = pl.multiple_of(n * tq, tq)
        return _dot_nt(qb, kb_ref[pl.ds(off, tq), :]) * ATT_SCALE, vb_ref[pl.ds(off, tq), :]

    s, v = scores(qi)
    s = jnp.where(col <= row, s, NEG_INF)
    m0 = jnp.max(s, axis=1, keepdims=True)
    p = jnp.exp(s - m0)
    l0 = jnp.sum(p, axis=1, keepdims=True)
    acc0 = _dot(p.astype(BF16), v)

    def past(n, carry):
        m_prev, l_prev, acc = carry
        s, v = scores(n)
        chosen = jnp.max(jnp.where(lane == n, sel_ref[...], 0.0), axis=1, keepdims=True) > 0.0
        s = jnp.where(chosen, s, NEG_INF)
        m_new = jnp.maximum(m_prev, jnp.max(s, axis=1, keepdims=True))
        alpha = jnp.exp(m_prev - m_new)
        p = jnp.exp(s - m_new)
        l_new = alpha * l_prev + jnp.sum(p, axis=1, keepdims=True)
        return m_new, l_new, alpha * acc + _dot(p.astype(BF16), v)

    _, l, acc = lax.fori_loop(0, qi, past, (m0, l0, acc0))
    o_ref[...] = acc / l


def _moba_prompt(q_all, k_all, v_all, batch, seq):
    tq = MOBA_BLOCK
    nq = seq // tq
    kv_spec = pl.BlockSpec((seq, D_HEAD), lambda b, h, i: (b, h))
    return pl.pallas_call(
        functools.partial(_moba_prompt_body, tq=tq, nb=nq),
        grid=(batch, N_HEADS, nq),
        in_specs=[pl.BlockSpec((tq, D_HEAD), lambda b, h, i: (b * nq + i, h)), kv_spec, kv_spec],
        out_specs=pl.BlockSpec((tq, D_HEAD), lambda b, h, i: (b * nq + i, h)),
        out_shape=jax.ShapeDtypeStruct((batch * seq, GROUP_W), F32),
        scratch_shapes=[pltpu.VMEM((seq, D_HEAD), BF16), pltpu.VMEM((seq, D_HEAD), BF16),
                        pltpu.VMEM((D_HEAD, D_HEAD), F32), pltpu.VMEM((tq, D_HEAD), F32)],
        compiler_params=_params("arbitrary", "arbitrary", "arbitrary"),
        name="moba_prompt",
    )(q_all, k_all, v_all)


def _moba_sample_body(pt_ref, q_ref, kn_ref, vn_ref, *refs, npages, t_new):
    kp, vp, o_ref = refs[:npages], refs[npages:2 * npages], refs[2 * npages]
    rows = N_HEADS * t_new
    ppb = MOBA_BLOCK // PAGE_SIZE
    nblk = npages // ppb
    q_st = _stack_heads(q_ref[...], t_new)
    qb = q_st.astype(BF16)

    gates = []
    for n in range(nblk):
        ksum = kp[n * ppb][...].sum(axis=0, keepdims=True)
        for j in range(1, ppb):
            ksum = ksum + kp[n * ppb + j][...].sum(axis=0, keepdims=True)
        gates.append(jnp.sum(q_st * (ksum * (1.0 / MOBA_BLOCK)), axis=1, keepdims=True))
    chosen = []
    for n in range(nblk):
        rank = jnp.zeros((rows, 1), jnp.int32)
        for m in range(nblk):
            if m != n:
                beats = (gates[m] > gates[n]) | ((gates[m] == gates[n]) & (m < n))
                rank = rank + beats.astype(jnp.int32)
        chosen.append(rank < MOBA_TOPK)

    col = lax.broadcasted_iota(jnp.int32, (rows, PAGE_SIZE), 1)
    kn = _pad_rows(kn_ref[...], PAGE_SIZE).astype(BF16)
    s_own = jnp.where(col <= _row_time(t_new, PAGE_SIZE), _dot_nt(qb, kn) * ATT_SCALE, NEG_INF)
    s_pages = []
    for p in range(npages):
        s = _dot_nt(qb, kp[p][...].astype(BF16)) * ATT_SCALE
        s_pages.append(jnp.where(chosen[p // ppb], s, NEG_INF))
    m = jnp.max(s_own, axis=1, keepdims=True)
    for s in s_pages:
        m = jnp.maximum(m, jnp.max(s, axis=1, keepdims=True))
    p_own = jnp.exp(s_own - m)
    l = jnp.sum(p_own, axis=1, keepdims=True)
    acc = _dot(p_own.astype(BF16), _pad_rows(vn_ref[...], PAGE_SIZE).astype(BF16))
    for p in range(npages):
        pr = jnp.exp(s_pages[p] - m)
        l = l + jnp.sum(pr, axis=1, keepdims=True)
        acc = acc + _dot(pr.astype(BF16), vp[p][...].astype(BF16))
    o_ref[...] = _unstack_heads(acc / l, t_new)


def _moba_sample(layer, page_table, q_all, k_all, v_all, cache_k, cache_v, row0, nseq, t_new):
    npages = page_table.shape[1]
    blk0 = row0 // t_new
    new_spec = pl.BlockSpec((t_new, GROUP_W), lambda b, pt: (blk0 + b, 0))
    grid_spec = pltpu.PrefetchScalarGridSpec(
        num_scalar_prefetch=1,
        grid=(nseq,),
        in_specs=[new_spec, new_spec, new_spec] + _paged_specs(layer, npages) + _paged_specs(layer, npages),
        out_specs=pl.BlockSpec((t_new, GROUP_W), lambda b, pt: (b, 0)),
    )
    return pl.pallas_call(
        functools.partial(_moba_sample_body, npages=npages, t_new=t_new),
        grid_spec=grid_spec,
        out_shape=jax.ShapeDtypeStruct((nseq * t_new, GROUP_W), F32),
        compiler_params=_params("arbitrary"),
        name="moba_sample",
    )(page_table.reshape(-1), q_all, k_all, v_all, *([cache_k] * npages), *([cache_v] * npages))


def _softmax_pv(s, v):
    m = jnp.max(s, axis=1, keepdims=True)
    p = jnp.exp(s - m)
    l = jnp.sum(p, axis=1, keepdims=True)
    return _dot(p.astype(BF16), v) / l


def _mem_prompt_body(q_ref, mk_ref, mv_ref, o_ref):
    s = _dot_nt(q_ref[...].astype(BF16), mk_ref[...].astype(BF16)) * ATT_SCALE
    o_ref[...] = _softmax_pv(s, mv_ref[...].astype(BF16))


def _mem_prompt(q_all, mk, mv, batch, seq, tq):
    nq = seq // tq
    n_mem = mk.shape[0] // batch
    kv_spec = pl.BlockSpec((n_mem, D_HEAD), lambda b, h, i: (b, h))
    return pl.pallas_call(
        _mem_prompt_body,
        grid=(batch, N_HEADS, nq),
        in_specs=[pl.BlockSpec((tq, D_HEAD), lambda b, h, i: (b * nq + i, h)), kv_spec, kv_spec],
        out_specs=pl.BlockSpec((tq, D_HEAD), lambda b, h, i: (b * nq + i, h)),
        out_shape=jax.ShapeDtypeStruct((batch * seq, GROUP_W), F32),
        compiler_params=_params("arbitrary", "arbitrary", "arbitrary"),
        name="mem_prompt",
    )(q_all, mk, mv)


def _mem_sample_body(q_ref, mk_ref, mv_ref, o_ref, *, nb, t_new):
    for i in range(nb):
        qb = _stack_heads(q_ref[i * t_new:(i + 1) * t_new, :], t_new).astype(BF16)
        s = _dot_nt(qb, mk_ref[i].astype(BF16)) * ATT_SCALE
        o_ref[i * t_new:(i + 1) * t_new, :] = _unstack_heads(_softmax_pv(s, mv_ref[i].astype(BF16)), t_new)


def _mem_sample(layer, q_all, mem_k, mem_v, row0, nseq, t_new, nb):
    n_mem = mem_k.shape[2]
    blk0 = row0 // (nb * t_new)
    kv_spec = pl.BlockSpec((None, nb, n_mem, GROUP_W), lambda i: (layer, i, 0, 0))
    return pl.pallas_call(
        functools.partial(_mem_sample_body, nb=nb, t_new=t_new),
        grid=(nseq // nb,),
        in_specs=[pl.BlockSpec((nb * t_new, GROUP_W), lambda i: (blk0 + i, 0)), kv_spec, kv_spec],
        out_specs=pl.BlockSpec((nb * t_new, GROUP_W), lambda i: (i, 0)),
        out_shape=jax.ShapeDtypeStruct((nseq * t_new, GROUP_W), F32),
        compiler_params=_params("arbitrary"),
        name="mem_sample",
    )(q_all, mem_k, mem_v)


def _finish_attn_body(y0, y1, y2, y3, x_ref, w_ref, g_ref, b_ref, o_ref, *, alpha):
    acc = alpha * x_ref[...]
    for gi, y_ref in enumerate((y0, y1, y2, y3)):
        acc = acc + _dot(y_ref[...].astype(BF16), w_ref[gi * GROUP_W:(gi + 1) * GROUP_W, :])
    o_ref[...] = _layer_norm(acc, g_ref[...], b_ref[...])


def _finish_attn(parts, x, w_bf16, g, b, alpha, tm):
    rows, d_model = x.shape
    part_spec = pl.BlockSpec((tm, GROUP_W), lambda i: (i, 0))
    vec_spec = pl.BlockSpec((1, d_model), lambda i: (0, 0))
    return pl.pallas_call(
        functools.partial(_finish_attn_body, alpha=alpha),
        grid=(rows // tm,),
        in_specs=[part_spec] * 4 + [pl.BlockSpec((tm, d_model), lambda i: (i, 0)),
                                    pl.BlockSpec(w_bf16.shape, lambda i: (0, 0)), vec_spec, vec_spec],
        out_specs=pl.BlockSpec((tm, d_model), lambda i: (i, 0)),
        out_shape=jax.ShapeDtypeStruct((rows, d_model), F32),
        compiler_params=_params("arbitrary"),
        name="out_projection_ln",
    )(*parts, x, w_bf16, g, b)


def _ffn_body(x_ref, wu_ref, wd_ref, g_ref, b_ref, o_ref, xb_ref, acc_ref, *, alpha):
    j = pl.program_id(1)

    @pl.when(j == 0)
    def _init():
        xb_ref[...] = x_ref[...].astype(BF16)
        acc_ref[...] = jnp.zeros(acc_ref.shape, F32)

    hid = jnp.square(jnp.maximum(_dot(xb_ref[...], wu_ref[...]), 0.0))
    acc_ref[...] += _dot(hid.astype(BF16), wd_ref[...])

    @pl.when(j == pl.num_programs(1) - 1)
    def _finish():
        o_ref[...] = _layer_norm(alpha * x_ref[...] + acc_ref[...], g_ref[...], b_ref[...])


def _ffn(x, wu_bf16, wd_bf16, g, b, alpha, tm, tf):
    rows, d_model = x.shape
    d_ff = wu_bf16.shape[1]
    vec_spec = pl.BlockSpec((1, d_model), lambda i, j: (0, 0))
    return pl.pallas_call(
        functools.partial(_ffn_body, alpha=alpha),
        grid=(rows // tm, d_ff // tf),
        in_specs=[pl.BlockSpec((tm, d_model), lambda i, j: (i, 0)),
                  pl.BlockSpec((d_model, tf), lambda i, j: (0, j)),
                  pl.BlockSpec((tf, d_model), lambda i, j: (j, 0)), vec_spec, vec_spec],
        out_specs=pl.BlockSpec((tm, d_model), lambda i, j: (i, 0)),
        out_shape=jax.ShapeDtypeStruct((rows, d_model), F32),
        scratch_shapes=[pltpu.VMEM((tm, d_model), BF16), pltpu.VMEM((tm, d_model), F32)],
        compiler_params=_params("arbitrary", "arbitrary"),
        name="ffn_ln",
    )(x, wu_bf16, wd_bf16, g, b)


def _rope_tables(positions):
    half = D_HEAD // 2
    inv = 1.0 / (ROPE_THETA ** (jnp.arange(half, dtype=F32) / half))
    ang = positions.astype(F32)[:, None] * inv[None, :]
    cos, sin = jnp.cos(ang), jnp.sin(ang)
    return jnp.concatenate([cos, cos], axis=1), jnp.concatenate([-sin, sin], axis=1)


def _largest_tile(n, cap):
    t = min(n, cap)
    while n % t:
        t //= 2
    return t


def kernel(x_prompt, x_sample, cache_sb_k, cache_sb_v, cache_mb_k, cache_mb_v, cache_mem_k, cache_mem_v, state_pool, page_table, mem_prompt, w_in, w_mem_k, w_mem_v, pool_w, pool_scale, w_out, ln1_g, ln1_b, w_up, w_down, ln2_g, ln2_b):
    batch, seq, d_model = x_prompt.shape
    nseq, t_new, _ = x_sample.shape
    depth = w_in.shape[0]
    npages = page_table.shape[1]
    past = npages * PAGE_SIZE
    n_mem = mem_prompt.shape[1]
    rows_p, rows_s = batch * seq, nseq * t_new
    rows = rows_p + rows_s
    alpha = float((2 * depth) ** 0.25)
    assert d_model == 4 * GROUP_W and w_in.shape[2] == N_GROUPS_IN * GROUP_W
    assert seq % MOBA_BLOCK == 0 and past % MOBA_BLOCK == 0 and t_new <= PAGE_SIZE and t_new % 8 == 0
    assert state_pool.shape[2] == POOL_BUF

    tm = _largest_tile(np.gcd(rows_p, rows_s), 512)
    n_phys = cache_sb_k.shape[1]
    paged = lambda c: c.reshape(depth, n_phys, PAGE_SIZE, GROUP_W)
    csk, csv, cmk, cmv = paged(cache_sb_k), paged(cache_sb_v), paged(cache_mb_k), paged(cache_mb_v)
    memk = cache_mem_k.reshape(depth, nseq, n_mem, GROUP_W)
    memv = cache_mem_v.reshape(depth, nseq, n_mem, GROUP_W)
    mem2d = mem_prompt.reshape(batch * n_mem, d_model)

    pos = jnp.concatenate([jnp.tile(jnp.arange(seq, dtype=jnp.int32), batch),
                           jnp.tile(past + jnp.arange(t_new, dtype=jnp.int32), nseq)])
    cos_tab, sin_tab = _rope_tables(pos)

    x = jnp.concatenate([x_prompt.reshape(rows_p, d_model), x_sample.reshape(rows_s, d_model)], axis=0)
    p_states, s_states = [], []
    for l in range(depth):
        u, q_sb, k_sb, v_sb, q_mb, k_mb, v_mb, q_mem = _in_projection(
            x, w_in[l].astype(BF16), cos_tab, sin_tab, tm)
        pw = pool_w[l].astype(BF16)
        psc = pool_scale[l].reshape(1, GROUP_W)
        u_s = u[rows_p:].reshape(nseq, t_new, GROUP_W)
        u_ext = jnp.concatenate([jnp.zeros((nseq, POOL_HALO - POOL_BUF, GROUP_W), F32), state_pool[l], u_s], axis=1)
        y_pool = jnp.concatenate([
            _pool_prompt(u, pw, psc, batch, seq, _largest_tile(seq, 512)),
            _pool_sample(u_ext, pw, psc, past, t_new)], axis=0)
        y_sb = jnp.concatenate([
            _sb_prompt(q_sb, k_sb, v_sb, batch, seq, 256),
            _sb_sample(l, page_table, q_sb, k_sb, v_sb, csk, csv, rows_p, nseq, t_new)], axis=0)
        y_mb = jnp.concatenate([
            _moba_prompt(q_mb, k_mb, v_mb, batch, seq),
            _moba_sample(l, page_table, q_mb, k_mb, v_mb, cmk, cmv, rows_p, nseq, t_new)], axis=0)
        mk = _matmul(mem2d, w_mem_k[l].astype(BF16))
        mv = _matmul(mem2d, w_mem_v[l].astype(BF16))
        y_mem = jnp.concatenate([
            _mem_prompt(q_mem, mk, mv, batch, seq, _largest_tile(seq, 512)),
            _mem_sample(l, q_mem, memk, memv, rows_p, nseq, t_new, _largest_tile(nseq, 8))], axis=0)
        x = _finish_attn((y_pool, y_sb, y_mb, y_mem), x, w_out[l].astype(BF16),
                         ln1_g[l].reshape(1, d_model), ln1_b[l].reshape(1, d_model), alpha, tm)
        x = _ffn(x, w_up[l].astype(BF16), w_down[l].astype(BF16),
                 ln2_g[l].reshape(1, d_model), ln2_b[l].reshape(1, d_model), alpha, tm,
                 _largest_tile(w_up.shape[2], 1024))

        heads_p = lambda a: a[:rows_p].reshape(batch, seq, N_HEADS, D_HEAD)
        heads_s = lambda a: a[rows_p:].reshape(nseq, t_new, N_HEADS, D_HEAD)
        mem_heads = lambda a: a.reshape(batch, n_mem, N_HEADS, D_HEAD)
        u_p = u[:rows_p].reshape(batch, seq, GROUP_W)
        p_states.append((heads_p(k_sb), heads_p(v_sb), heads_p(k_mb), heads_p(v_mb),
                         mem_heads(mk), mem_heads(mv), u_p[:, seq - POOL_BUF:]))
        s_states.append((heads_s(k_sb), heads_s(v_sb), heads_s(k_mb), heads_s(v_mb),
                         u_ext[:, u_ext.shape[1] - POOL_BUF:]))

    stack = lambda states, i: jnp.stack([st[i] for st in states], axis=0)
    return (x[:rows_p].reshape(batch, seq, d_model), x[rows_p:].reshape(nseq, t_new, d_model),
            *[stack(p_states, i) for i in range(7)], *[stack(s_states, i) for i in range(5)])
```

```python
import functools

import numpy as np
import jax
import jax.numpy as jnp
from jax import lax
from jax.experimental import pallas as pl
from jax.experimental.pallas import tpu as pltpu

F32 = jnp.float32
BF16 = jnp.bfloat16

D_HEAD = 128
N_HEADS = 4
GROUP_W = N_HEADS * D_HEAD
N_GROUPS_IN = 8
POOL_WINDOWS = (2, 4, 8, 16)
POOL_BUF = 15
POOL_HALO = 16
MOBA_BLOCK = 256
MOBA_TOPK = 3
PAGE_SIZE = 128
ROPE_THETA = 10000.0
LN_EPS = 1e-5
NEG_INF = -1e30
ATT_SCALE = D_HEAD ** -0.5
HEADS_PER_STEP = 2

VMEM_LIMIT_BYTES = 56 * 1024 * 1024


def _params(*sem):
    return pltpu.CompilerParams(dimension_semantics=sem, vmem_limit_bytes=VMEM_LIMIT_BYTES)


def _dot(a, b):
    return jnp.dot(a, b, preferred_element_type=F32)


def _dot_nt(a, b):
    return lax.dot_general(a, b, (((1,), (1,)), ((), ())), preferred_element_type=F32)


def _split_hi_lo(x):
    hi = x.astype(BF16)
    lo = (x - hi.astype(F32)).astype(BF16)
    return hi, lo


def _softplus(z):
    return jnp.maximum(z, 0.0) + jnp.log(1.0 + jnp.exp(-jnp.abs(z)))


def _layer_norm(x, g, b):
    mu = jnp.mean(x, axis=-1, keepdims=True)
    xc = x - mu
    var = jnp.mean(xc * xc, axis=-1, keepdims=True)
    return xc * lax.rsqrt(var + LN_EPS) * g + b


def _head_cols(x, h):
    return x[:, h * D_HEAD:(h + 1) * D_HEAD]


def _load_heads(ref, tokens, lead=()):
    return jnp.concatenate(
        [ref[lead + (pl.ds(h, tokens, stride=N_HEADS), slice(None))] for h in range(N_HEADS)], axis=1)


def _store_heads(ref, val):
    for h in range(N_HEADS):
        ref[pl.ds(h, val.shape[0], stride=N_HEADS), :] = _head_cols(val, h)


def _rope(h, cos, sin):
    return jnp.concatenate(
        [_head_cols(h, c) * cos + pltpu.roll(_head_cols(h, c), D_HEAD // 2, 1) * sin
         for c in range(N_HEADS)], axis=1)


def _inproj_body(x_ref, w_ref, cos_ref, sin_ref, *refs, nbp, tm):
    (u_ref, qsb_ref, qmb_ref, qmem_ref, ksb_ref, vsb_ref, kmb_ref, vmb_ref, kmean_ref,
     pksb_ref, pvsb_ref, pkmb_ref, pvmb_ref, sksb_ref, svsb_ref, skmb_ref, svmb_ref, xb_ref) = refs[-18:]
    i = pl.program_id(0)
    j = pl.program_id(1)

    @pl.when(j == 0)
    def _cast():
        xb_ref[...] = x_ref[...].astype(BF16)

    h = _dot(xb_ref[...], w_ref[...])

    def keep_kv(val, bf_ref, p_ref, s_ref):
        bf_ref[...] = val.astype(BF16)
        pl.when(i < nbp)(lambda: _store_heads(p_ref, val))
        pl.when(i >= nbp)(lambda: _store_heads(s_ref, val))

    @pl.when(j == 0)
    def _u():
        u_ref[...] = h

    @pl.when(j == 1)
    def _qsb():
        qsb_ref[...] = h

    @pl.when(j == 2)
    def _ksb():
        keep_kv(h, ksb_ref, pksb_ref, sksb_ref)

    @pl.when(j == 3)
    def _vsb():
        keep_kv(h, vsb_ref, pvsb_ref, svsb_ref)

    @pl.when(j == 4)
    def _qmb():
        qmb_ref[...] = _rope(h, cos_ref[...], sin_ref[...])

    @pl.when(j == 5)
    def _kmb():
        r = _rope(h, cos_ref[...], sin_ref[...])
        keep_kv(r, kmb_ref, pkmb_ref, skmb_ref)
        for n in range(tm // MOBA_BLOCK):
            blk = r[n * MOBA_BLOCK:(n + 1) * MOBA_BLOCK, :]
            kmean_ref[n:n + 1, :] = jnp.sum(blk, axis=0, keepdims=True) * (1.0 / MOBA_BLOCK)

    @pl.when(j == 6)
    def _vmb():
        keep_kv(h, vmb_ref, pvmb_ref, svmb_ref)

    @pl.when(j == 7)
    def _qmem():
        qmem_ref[...] = h


def _in_projection(layer, depth, x, w_bf16, cos_tab, sin_tab, tm, rows_p, prev_states):
    rows, d_model = x.shape
    rows_s = rows - rows_p
    nbp = rows_p // tm
    assert w_bf16.shape == (d_model, N_GROUPS_IN * GROUP_W)
    assert rows_p % tm == 0 and rows_s % tm == 0 and tm % MOBA_BLOCK == 0
    row_spec = pl.BlockSpec((tm, GROUP_W), lambda i, j: (i, 0))
    p_spec = pl.BlockSpec((None, tm * N_HEADS, D_HEAD), lambda i, j: (layer, jnp.minimum(i, nbp - 1), 0))
    s_spec = pl.BlockSpec((None, tm * N_HEADS, D_HEAD), lambda i, j: (layer, jnp.maximum(i - nbp, 0), 0))
    nkm = tm // MOBA_BLOCK
    f32_sds = jax.ShapeDtypeStruct((rows, GROUP_W), F32)
    bf_sds = jax.ShapeDtypeStruct((rows, GROUP_W), BF16)
    p_sds = jax.ShapeDtypeStruct((depth, rows_p * N_HEADS, D_HEAD), F32)
    s_sds = jax.ShapeDtypeStruct((depth, rows_s * N_HEADS, D_HEAD), F32)
    in_specs = [
        pl.BlockSpec((tm, d_model), lambda i, j: (i, 0)),
        pl.BlockSpec((d_model, GROUP_W), lambda i, j: (0, j)),
        pl.BlockSpec((tm, D_HEAD), lambda i, j: (i, 0)),
        pl.BlockSpec((tm, D_HEAD), lambda i, j: (i, 0)),
    ]
    args = [x, w_bf16, cos_tab, sin_tab]
    aliases = {}
    if prev_states is not None:
        in_specs += [pl.BlockSpec(memory_space=pl.ANY)] * 8
        args += list(prev_states)
        aliases = {4 + k: 9 + k for k in range(8)}
    outs = pl.pallas_call(
        functools.partial(_inproj_body, nbp=nbp, tm=tm),
        grid=(rows // tm, N_GROUPS_IN),
        in_specs=in_specs,
        out_specs=[row_spec] * 8 + [pl.BlockSpec((None, nkm, GROUP_W), lambda i, j: (i, 0, 0))]
        + [p_spec] * 4 + [s_spec] * 4,
        out_shape=[f32_sds] * 4 + [bf_sds] * 4
        + [jax.ShapeDtypeStruct((rows // tm, nkm, GROUP_W), F32)] + [p_sds] * 4 + [s_sds] * 4,
        scratch_shapes=[pltpu.VMEM((tm, d_model), BF16)],
        input_output_aliases=aliases,
        compiler_params=_params("arbitrary", "arbitrary"),
        name="in_projection",
    )(*args)
    return outs[0:4], outs[4:8], outs[8].reshape(rows // MOBA_BLOCK, GROUP_W), outs[9:17]


def _mm_body(x_ref, w_ref, o_ref):
    o_ref[...] = _dot(x_ref[...].astype(BF16), w_ref[...])


def _matmul(x, w_bf16):
    m, k = x.shape
    n = w_bf16.shape[1]
    return pl.pallas_call(
        _mm_body,
        grid=(1,),
        in_specs=[pl.BlockSpec((m, k), lambda i: (0, 0)), pl.BlockSpec((k, n), lambda i: (0, 0))],
        out_specs=pl.BlockSpec((m, n), lambda i: (0, 0)),
        out_shape=jax.ShapeDtypeStruct((m, n), F32),
        compiler_params=_params("arbitrary"),
        name="mem_projection",
    )(x, w_bf16)


def _pool_finish(sums, u_groups, counts, pw_ref, sc_ref):
    ys = []
    for g in range(len(POOL_WINDOWS)):
        mixed = sums[g] / counts[g] - u_groups[g]
        ys.append(_dot(mixed.astype(BF16), pw_ref[g]))
    return jnp.concatenate(ys, axis=1) * sc_ref[...]


def _pool_prompt_body(u_ref, pw_ref, sc_ref, o_ref, ext_ref, *, tt):
    i = pl.program_id(1)
    gd = GROUP_W // len(POOL_WINDOWS)

    @pl.when(i == 0)
    def _reset():
        ext_ref[0:POOL_HALO, :] = jnp.zeros((POOL_HALO, GROUP_W), F32)

    ext_ref[POOL_HALO:POOL_HALO + tt, :] = u_ref[...]
    sums = [None] * len(POOL_WINDOWS)
    for s in range(max(POOL_WINDOWS)):
        g0 = min(g for g, w in enumerate(POOL_WINDOWS) if s < w)
        sh = ext_ref[pl.ds(POOL_HALO - s, tt), pl.ds(g0 * gd, GROUP_W - g0 * gd)]
        for g in range(g0, len(POOL_WINDOWS)):
            piece = sh[:, (g - g0) * gd:(g - g0 + 1) * gd]
            sums[g] = piece if sums[g] is None else sums[g] + piece
    pos = i * tt + lax.broadcasted_iota(jnp.int32, (tt, gd), 0)
    counts = [jnp.minimum(w, pos + 1).astype(F32) for w in POOL_WINDOWS]
    u = u_ref[...]
    u_groups = [u[:, g * gd:(g + 1) * gd] for g in range(len(POOL_WINDOWS))]
    o_ref[...] = _pool_finish(sums, u_groups, counts, pw_ref, sc_ref)
    ext_ref[0:POOL_HALO, :] = ext_ref[tt:tt + POOL_HALO, :]


def _pool_prompt(u_all, pw_bf16, scale, batch, seq, tt):
    nt = seq // tt
    return pl.pallas_call(
        functools.partial(_pool_prompt_body, tt=tt),
        grid=(batch, nt),
        in_specs=[
            pl.BlockSpec((tt, GROUP_W), lambda b, i: (b * nt + i, 0)),
            pl.BlockSpec(pw_bf16.shape, lambda b, i: (0, 0, 0)),
            pl.BlockSpec((1, GROUP_W), lambda b, i: (0, 0)),
        ],
        out_specs=pl.BlockSpec((tt, GROUP_W), lambda b, i: (b * nt + i, 0)),
        out_shape=jax.ShapeDtypeStruct(u_all.shape, F32),
        scratch_shapes=[pltpu.VMEM((POOL_HALO + tt, GROUP_W), F32)],
        compiler_params=_params("arbitrary", "arbitrary"),
        name="pool_prompt",
    )(u_all, pw_bf16, scale)


def _pool_sample_body(ue_ref, pw_ref, sc_ref, y_hbm, o_ref, *, past, t_new):
    del y_hbm
    nseq = ue_ref.shape[0]
    gd = GROUP_W // len(POOL_WINDOWS)
    sums = [None] * len(POOL_WINDOWS)
    for s in range(max(POOL_WINDOWS)):
        g0 = min(g for g, w in enumerate(POOL_WINDOWS) if s < w)
        sh = ue_ref[:, pl.ds(POOL_HALO - s, t_new), pl.ds(g0 * gd, GROUP_W - g0 * gd)]
        for g in range(g0, len(POOL_WINDOWS)):
            piece = sh[:, :, (g - g0) * gd:(g - g0 + 1) * gd].reshape(nseq * t_new, gd)
            sums[g] = piece if sums[g] is None else sums[g] + piece
    pos = past + lax.broadcasted_iota(jnp.int32, (nseq, t_new, gd), 1).reshape(nseq * t_new, gd)
    counts = [jnp.minimum(w, pos + 1).astype(F32) for w in POOL_WINDOWS]
    u = ue_ref[:, pl.ds(POOL_HALO, t_new), :]
    u_groups = [u[:, :, g * gd:(g + 1) * gd].reshape(nseq * t_new, gd) for g in range(len(POOL_WINDOWS))]
    o_ref[...] = _pool_finish(sums, u_groups, counts, pw_ref, sc_ref)


def _pool_sample(u_ext, pw_bf16, scale, y_all, past, t_new, rows_p):
    nseq = u_ext.shape[0]
    rows_s = nseq * t_new
    assert rows_p % rows_s == 0
    return pl.pallas_call(
        functools.partial(_pool_sample_body, past=past, t_new=t_new),
        grid=(1,),
        in_specs=[
            pl.BlockSpec(u_ext.shape, lambda i: (0, 0, 0)),
            pl.BlockSpec(pw_bf16.shape, lambda i: (0, 0, 0)),
            pl.BlockSpec((1, GROUP_W), lambda i: (0, 0)),
            pl.BlockSpec(memory_space=pl.ANY),
        ],
        out_specs=pl.BlockSpec((rows_s, GROUP_W), lambda i: (rows_p // rows_s, 0)),
        out_shape=jax.ShapeDtypeStruct(y_all.shape, F32),
        input_output_aliases={3: 0},
        compiler_params=_params("arbitrary"),
        name="pool_sample",
    )(u_ext, pw_bf16, scale, y_all)


def _suffix_sum_matrix(n):
    row = lax.broadcasted_iota(jnp.int32, (n, n), 0)
    col = lax.broadcasted_iota(jnp.int32, (n, n), 1)
    upper = jnp.where(row > col, 1.0, 0.0).astype(BF16)
    return jnp.concatenate([upper, jnp.ones((n, n), BF16)], axis=1)


def _sb_blocks(qs, ks, vs, u2, carries, valid):
    n = ks[0].shape[0]
    zs = [_dot_nt(q, k) * ATT_SCALE for q, k in zip(qs, ks)]
    sps = [_softplus(z) for z in zs]
    log_fails = [-sp if valid is None else jnp.where(valid, -sp, 0.0) for sp in sps]
    csts = [_dot(lf.astype(BF16), u2) for lf in log_fails]
    ws = [jnp.exp(z - sp + cst[:, :n] + c) for z, sp, cst, c in zip(zs, sps, csts, carries)]
    if valid is not None:
        ws = [jnp.where(valid, w, 0.0) for w in ws]
    pvs = [_dot(w.astype(BF16), v) for w, v in zip(ws, vs)]
    return pvs, [c + cst[:, n:] for c, cst in zip(carries, csts)]


def _sb_prompt_body(q_ref, k_ref, v_ref, o_ref, acc_ref, car_ref, *, tq, hp):
    qi = pl.program_id(2)
    qs = [_head_cols(q_ref, h).astype(BF16) for h in range(hp)]
    u2 = _suffix_sum_matrix(tq)
    row = lax.broadcasted_iota(jnp.int32, (tq, tq), 0)
    col = lax.broadcasted_iota(jnp.int32, (tq, tq), 1)

    def blocks(kb, carries, valid):
        off = pl.multiple_of(kb * tq, tq)
        ks = [k_ref[pl.ds(off, tq), pl.ds(h * D_HEAD, D_HEAD)] for h in range(hp)]
        vs = [v_ref[pl.ds(off, tq), pl.ds(h * D_HEAD, D_HEAD)] for h in range(hp)]
        return _sb_blocks(qs, ks, vs, u2, carries, valid)

    pvs, cars = blocks(qi, [jnp.zeros((tq, tq), F32)] * hp, col < row)
    for h in range(hp):
        acc_ref[h] = pvs[h]
        car_ref[h] = cars[h]

    def past(t, c):
        pvs, cars = blocks(qi - 1 - t, [car_ref[h] for h in range(hp)], None)
        for h in range(hp):
            acc_ref[h] += pvs[h]
            car_ref[h] = cars[h]
        return c

    lax.fori_loop(0, qi, past, 0)
    o_ref[...] = jnp.concatenate([acc_ref[h] for h in range(hp)], axis=1)


def _sb_prompt(q_all, k_bf, v_bf, batch, seq, tq, hp):
    nq = seq // tq
    wide = hp * D_HEAD
    kv_spec = pl.BlockSpec((seq, wide), lambda b, h, i: (b, h))
    return pl.pallas_call(
        functools.partial(_sb_prompt_body, tq=tq, hp=hp),
        grid=(batch, N_HEADS // hp, nq),
        in_specs=[pl.BlockSpec((tq, wide), lambda b, h, i: (b * nq + i, h)), kv_spec, kv_spec],
        out_specs=pl.BlockSpec((tq, wide), lambda b, h, i: (b * nq + i, h)),
        out_shape=jax.ShapeDtypeStruct(q_all.shape, F32),
        scratch_shapes=[pltpu.VMEM((hp, tq, D_HEAD), F32), pltpu.VMEM((hp, tq, tq), F32)],
        compiler_params=_params("arbitrary", "arbitrary", "arbitrary"),
        name="sb_prompt",
    )(q_all, k_bf, v_bf)


def _head_masks(t_new):
    rows = N_HEADS * t_new
    r = lax.broadcasted_iota(jnp.int32, (rows, GROUP_W), 0)
    c = lax.broadcasted_iota(jnp.int32, (rows, GROUP_W), 1)
    same = None
    for h in range(N_HEADS):
        m = (r >= h * t_new) & (r < (h + 1) * t_new) & (c >= h * D_HEAD) & (c < (h + 1) * D_HEAD)
        same = m if same is None else same | m
    return same


def _stack_heads(q, t_new):
    qrep = jnp.concatenate([q] * N_HEADS, axis=0)
    return jnp.where(_head_masks(t_new), qrep, 0.0)


def _unstack_heads(acc, t_new):
    masked = jnp.where(_head_masks(t_new), acc, 0.0)
    out = masked[0:t_new]
    for h in range(1, N_HEADS):
        out = out + masked[h * t_new:(h + 1) * t_new]
    return out


def _row_time(t_new, width):
    r = lax.broadcasted_iota(jnp.int32, (N_HEADS * t_new, width), 0)
    t = r
    for h in range(1, N_HEADS):
        t = jnp.where(r >= h * t_new, r - h * t_new, t)
    return t


def _pad_rows(x, rows):
    return jnp.concatenate([x, jnp.zeros((rows - x.shape[0], x.shape[1]), x.dtype)], axis=0)


def _sb_sample_body(pt_ref, q_ref, kn_ref, vn_ref, *refs, npages, t_new):
    kp, vp, o_ref = refs[:npages], refs[npages:2 * npages], refs[-1]
    rows = N_HEADS * t_new
    qb = _stack_heads(q_ref[...], t_new).astype(BF16)
    u2 = _suffix_sum_matrix(PAGE_SIZE)
    col = lax.broadcasted_iota(jnp.int32, (rows, PAGE_SIZE), 1)
    valid_new = col < _row_time(t_new, PAGE_SIZE)
    ks = [_pad_rows(_load_heads(kn_ref, t_new), PAGE_SIZE).astype(BF16)]
    ks += [_load_heads(kp[p], PAGE_SIZE).astype(BF16) for p in range(npages - 1, -1, -1)]
    zs = [_dot_nt(qb, k) * ATT_SCALE for k in ks]
    sps = [_softplus(z) for z in zs]
    log_fails = [jnp.where(valid_new, -sps[0], 0.0)] + [-sp for sp in sps[1:]]
    cst = _dot(jnp.concatenate(log_fails, axis=0).astype(BF16), u2)
    carry = jnp.zeros((rows, PAGE_SIZE), F32)
    acc = None
    for b in range(npages + 1):
        cb = cst[b * rows:(b + 1) * rows]
        w = jnp.exp(zs[b] - sps[b] + cb[:, :PAGE_SIZE] + carry)
        carry = carry + cb[:, PAGE_SIZE:]
        if b == 0:
            w = jnp.where(valid_new, w, 0.0)
            v = _pad_rows(_load_heads(vn_ref, t_new), PAGE_SIZE)
        else:
            v = _load_heads(vp[npages - b], PAGE_SIZE)
        pv = _dot(w.astype(BF16), v.astype(BF16))
        acc = pv if acc is None else acc + pv
    o_ref[...] = _unstack_heads(acc, t_new)


def _paged_specs(layer, npages):
    def spec(p):
        return pl.BlockSpec((None, None, PAGE_SIZE * N_HEADS, D_HEAD),
                            lambda b, pt: (layer, pt[b * npages + p], 0, 0))
    return [spec(p) for p in range(npages)]


def _paged_sample_call(body, name, layer, page_table, q_all, k_new, v_new, cache_k, cache_v, y_all,
                       rows_p, nseq, t_new):
    npages = page_table.shape[1]
    blk0 = rows_p // t_new
    q_spec = pl.BlockSpec((t_new, GROUP_W), lambda b, pt: (blk0 + b, 0))
    new_spec = pl.BlockSpec((None, t_new * N_HEADS, D_HEAD), lambda b, pt: (layer, b, 0))
    grid_spec = pltpu.PrefetchScalarGridSpec(
        num_scalar_prefetch=1,
        grid=(nseq,),
        in_specs=[q_spec, new_spec, new_spec] + _paged_specs(layer, npages) + _paged_specs(layer, npages)
        + [pl.BlockSpec(memory_space=pl.ANY)],
        out_specs=pl.BlockSpec((t_new, GROUP_W), lambda b, pt: (blk0 + b, 0)),
    )
    n_in = 4 + 2 * npages
    return pl.pallas_call(
        functools.partial(body, npages=npages, t_new=t_new),
        grid_spec=grid_spec,
        out_shape=jax.ShapeDtypeStruct(y_all.shape, F32),
        input_output_aliases={n_in: 0},
        compiler_params=_params("arbitrary"),
        name=name,
    )(page_table.reshape(-1), q_all, k_new, v_new, *([cache_k] * npages), *([cache_v] * npages), y_all)


def _moba_prompt_body(q_ref, k_ref, v_ref, km_ref, o_ref, sel_ref, *, tq, nb, hp):
    qi = pl.program_id(2)
    lane = lax.broadcasted_iota(jnp.int32, (tq, D_HEAD), 1)
    past_blk = lane < qi
    qbs = []
    for h in range(hp):
        q = _head_cols(q_ref, h)
        qbs.append(q.astype(BF16))
        q_hi, q_lo = _split_hi_lo(q)
        km_hi, km_lo = _split_hi_lo(_pad_rows(_head_cols(km_ref, h), D_HEAD))
        gate = _dot_nt(q_hi, km_hi) + _dot_nt(q_lo, km_hi) + _dot_nt(q_hi, km_lo)
        gate = jnp.where(past_blk, gate, NEG_INF)
        rank = jnp.zeros(gate.shape, jnp.int32)
        for m in range(nb):
            gm = gate[:, m:m + 1]
            beats = (gm > gate) | ((gm == gate) & (m < lane))
            rank = rank + beats.astype(jnp.int32)
        sel_ref[h] = jnp.where(past_blk & (rank < MOBA_TOPK), 1.0, 0.0)

    row = lax.broadcasted_iota(jnp.int32, (tq, tq), 0)
    col = lax.broadcasted_iota(jnp.int32, (tq, tq), 1)

    def scores(n):
        off = pl.multiple_of(n * tq, tq)
        ss = [_dot_nt(qbs[h], k_ref[pl.ds(off, tq), pl.ds(h * D_HEAD, D_HEAD)]) * ATT_SCALE for h in range(hp)]
        vs = [v_ref[pl.ds(off, tq), pl.ds(h * D_HEAD, D_HEAD)] for h in range(hp)]
        return ss, vs

    ss, vs = scores(qi)
    ss = [jnp.where(col <= row, s, NEG_INF) for s in ss]
    ms = [jnp.max(s, axis=1, keepdims=True) for s in ss]
    ps = [jnp.exp(s - m) for s, m in zip(ss, ms)]
    ls = [jnp.sum(p, axis=1, keepdims=True) for p in ps]
    accs = [_dot(p.astype(BF16), v) for p, v in zip(ps, vs)]

    def past(n, carry):
        ms, ls, accs = carry
        ss, vs = scores(n)
        chosen = [jnp.max(jnp.where(lane == n, sel_ref[h], 0.0), axis=1, keepdims=True) > 0.0 for h in range(hp)]
        ss = [jnp.where(c, s, NEG_INF) for c, s in zip(chosen, ss)]
        m_new = [jnp.maximum(m, jnp.max(s, axis=1, keepdims=True)) for m, s in zip(ms, ss)]
        alphas = [jnp.exp(m - mn) for m, mn in zip(ms, m_new)]
        ps = [jnp.exp(s - mn) for s, mn in zip(ss, m_new)]
        ls = [a * l + jnp.sum(p, axis=1, keepdims=True) for a, l, p in zip(alphas, ls, ps)]
        accs = [a * acc + _dot(p.astype(BF16), v) for a, acc, p, v in zip(alphas, accs, ps, vs)]
        return m_new, ls, accs

    _, ls, accs = lax.fori_loop(0, qi, past, (ms, ls, accs))
    o_ref[...] = jnp.concatenate([acc / l for acc, l in zip(accs, ls)], axis=1)


def _moba_prompt(q_all, k_bf, v_bf, kmean, batch, seq, hp):
    tq = MOBA_BLOCK
    nq = seq // tq
    wide = hp * D_HEAD
    kv_spec = pl.BlockSpec((seq, wide), lambda b, h, i: (b, h))
    return pl.pallas_call(
        functools.partial(_moba_prompt_body, tq=tq, nb=nq, hp=hp),
        grid=(batch, N_HEADS // hp, nq),
        in_specs=[pl.BlockSpec((tq, wide), lambda b, h, i: (b * nq + i, h)), kv_spec, kv_spec,
                  pl.BlockSpec((nq, wide), lambda b, h, i: (b, h))],
        out_specs=pl.BlockSpec((tq, wide), lambda b, h, i: (b * nq + i, h)),
        out_shape=jax.ShapeDtypeStruct(q_all.shape, F32),
        scratch_shapes=[pltpu.VMEM((hp, tq, D_HEAD), F32)],
        compiler_params=_params("arbitrary", "arbitrary", "arbitrary"),
        name="moba_prompt",
    )(q_all, k_bf, v_bf, kmean)


def _moba_sample_body(pt_ref, q_ref, kn_ref, vn_ref, *refs, npages, t_new):
    kp, vp, o_ref = refs[:npages], refs[npages:2 * npages], refs[-1]
    rows = N_HEADS * t_new
    ppb = MOBA_BLOCK // PAGE_SIZE
    nblk = npages // ppb
    q_st = _stack_heads(q_ref[...], t_new)
    qb = q_st.astype(BF16)

    k_pages = [_load_heads(kp[p], PAGE_SIZE) for p in range(npages)]
    gates = []
    for n in range(nblk):
        ksum = k_pages[n * ppb].sum(axis=0, keepdims=True)
        for j in range(1, ppb):
            ksum = ksum + k_pages[n * ppb + j].sum(axis=0, keepdims=True)
        gates.append(jnp.sum(q_st * (ksum * (1.0 / MOBA_BLOCK)), axis=1, keepdims=True))
    chosen = []
    for n in range(nblk):
        rank = jnp.zeros((rows, 1), jnp.int32)
        for m in range(nblk):
            if m != n:
                beats = (gates[m] > gates[n]) | ((gates[m] == gates[n]) & (m < n))
                rank = rank + beats.astype(jnp.int32)
        chosen.append(rank < MOBA_TOPK)

    col = lax.broadcasted_iota(jnp.int32, (rows, PAGE_SIZE), 1)
    kn = _pad_rows(_load_heads(kn_ref, t_new), PAGE_SIZE).astype(BF16)
    s_own = jnp.where(col <= _row_time(t_new, PAGE_SIZE), _dot_nt(qb, kn) * ATT_SCALE, NEG_INF)
    s_pages = []
    for p in range(npages):
        s = _dot_nt(qb, k_pages[p].astype(BF16)) * ATT_SCALE
        s_pages.append(jnp.where(chosen[p // ppb], s, NEG_INF))
    m = jnp.max(s_own, axis=1, keepdims=True)
    for s in s_pages:
        m = jnp.maximum(m, jnp.max(s, axis=1, keepdims=True))
    p_own = jnp.exp(s_own - m)
    l = jnp.sum(p_own, axis=1, keepdims=True)
    acc = _dot(p_own.astype(BF16), _pad_rows(_load_heads(vn_ref, t_new), PAGE_SIZE).astype(BF16))
    for p in range(npages):
        pr = jnp.exp(s_pages[p] - m)
        l = l + jnp.sum(pr, axis=1, keepdims=True)
        acc = acc + _dot(pr.astype(BF16), _load_heads(vp[p], PAGE_SIZE).astype(BF16))
    o_ref[...] = _unstack_heads(acc / l, t_new)


def _softmax_pv(s, v):
    m = jnp.max(s, axis=1, keepdims=True)
    p = jnp.exp(s - m)
    l = jnp.sum(p, axis=1, keepdims=True)
    return _dot(p.astype(BF16), v) / l


def _mem_prompt_body(q_ref, mk_ref, mv_ref, o_ref):
    s = _dot_nt(q_ref[...].astype(BF16), mk_ref[...].astype(BF16)) * ATT_SCALE
    o_ref[...] = _softmax_pv(s, mv_ref[...].astype(BF16))


def _mem_prompt(q_all, mk, mv, batch, seq, tq):
    nq = seq // tq
    n_mem = mk.shape[0] // batch
    kv_spec = pl.BlockSpec((n_mem, D_HEAD), lambda b, h, i: (b, h))
    return pl.pallas_call(
        _mem_prompt_body,
        grid=(batch, N_HEADS, nq),
        in_specs=[pl.BlockSpec((tq, D_HEAD), lambda b, h, i: (b * nq + i, h)), kv_spec, kv_spec],
        out_specs=pl.BlockSpec((tq, D_HEAD), lambda b, h, i: (b * nq + i, h)),
        out_shape=jax.ShapeDtypeStruct(q_all.shape, F32),
        compiler_params=_params("arbitrary", "arbitrary", "arbitrary"),
        name="mem_prompt",
    )(q_all, mk, mv)


def _mem_sample_body(q_ref, mk_ref, mv_ref, y_hbm, o_ref, *, nb, t_new, n_mem):
    del y_hbm
    for i in range(nb):
        qb = _stack_heads(q_ref[i * t_new:(i + 1) * t_new, :], t_new).astype(BF16)
        s = _dot_nt(qb, _load_heads(mk_ref, n_mem, (i,)).astype(BF16)) * ATT_SCALE
        pv = _softmax_pv(s, _load_heads(mv_ref, n_mem, (i,)).astype(BF16))
        o_ref[i * t_new:(i + 1) * t_new, :] = _unstack_heads(pv, t_new)


def _mem_sample(layer, q_all, mem_k, mem_v, y_all, rows_p, nseq, t_new, nb):
    n_mem = mem_k.shape[2] // N_HEADS
    blk0 = rows_p // (nb * t_new)
    kv_spec = pl.BlockSpec((None, nb, n_mem * N_HEADS, D_HEAD), lambda i: (layer, i, 0, 0))
    return pl.pallas_call(
        functools.partial(_mem_sample_body, nb=nb, t_new=t_new, n_mem=n_mem),
        grid=(nseq // nb,),
        in_specs=[pl.BlockSpec((nb * t_new, GROUP_W), lambda i: (blk0 + i, 0)), kv_spec, kv_spec,
                  pl.BlockSpec(memory_space=pl.ANY)],
        out_specs=pl.BlockSpec((nb * t_new, GROUP_W), lambda i: (blk0 + i, 0)),
        out_shape=jax.ShapeDtypeStruct(y_all.shape, F32),
        input_output_aliases={3: 0},
        compiler_params=_params("arbitrary"),
        name="mem_sample",
    )(q_all, mem_k, mem_v, y_all)


def _finish_attn_body(y0, y1, y2, y3, x_ref, w_ref, g_ref, b_ref, o_ref, *, alpha):
    acc = alpha * x_ref[...]
    for gi, y_ref in enumerate((y0, y1, y2, y3)):
        acc = acc + _dot(y_ref[...].astype(BF16), w_ref[gi * GROUP_W:(gi + 1) * GROUP_W, :])
    o_ref[...] = _layer_norm(acc, g_ref[...], b_ref[...])


def _finish_attn(parts, x, w_bf16, g, b, alpha, tm):
    rows, d_model = x.shape
    part_spec = pl.BlockSpec((tm, GROUP_W), lambda i: (i, 0))
    vec_spec = pl.BlockSpec((1, d_model), lambda i: (0, 0))
    return pl.pallas_call(
        functools.partial(_finish_attn_body, alpha=alpha),
        grid=(rows // tm,),
        in_specs=[part_spec] * 4 + [pl.BlockSpec((tm, d_model), lambda i: (i, 0)),
                                    pl.BlockSpec(w_bf16.shape, lambda i: (0, 0)), vec_spec, vec_spec],
        out_specs=pl.BlockSpec((tm, d_model), lambda i: (i, 0)),
        out_shape=jax.ShapeDtypeStruct((rows, d_model), F32),
        compiler_params=_params("arbitrary"),
        name="out_projection_ln",
    )(*parts, x, w_bf16, g, b)


def _ffn_body(x_ref, wu_ref, wd_ref, g_ref, b_ref, o_ref, xb_ref, acc_ref, *, alpha):
    j = pl.program_id(1)

    @pl.when(j == 0)
    def _init():
        xb_ref[...] = x_ref[...].astype(BF16)
        acc_ref[...] = jnp.zeros(acc_ref.shape, F32)

    hid = jnp.square(jnp.maximum(_dot(xb_ref[...], wu_ref[...]), 0.0))
    acc_ref[...] += _dot(hid.astype(BF16), wd_ref[...])

    @pl.when(j == pl.num_programs(1) - 1)
    def _finish():
        o_ref[...] = _layer_norm(alpha * x_ref[...] + acc_ref[...], g_ref[...], b_ref[...])


def _ffn(x, wu_bf16, wd_bf16, g, b, alpha, tm, tf):
    rows, d_model = x.shape
    d_ff = wu_bf16.shape[1]
    vec_spec = pl.BlockSpec((1, d_model), lambda i, j: (0, 0))
    return pl.pallas_call(
        functools.partial(_ffn_body, alpha=alpha),
        grid=(rows // tm, d_ff // tf),
        in_specs=[pl.BlockSpec((tm, d_model), lambda i, j: (i, 0)),
                  pl.BlockSpec((d_model, tf), lambda i, j: (0, j)),
                  pl.BlockSpec((tf, d_model), lambda i, j: (j, 0)), vec_spec, vec_spec],
        out_specs=pl.BlockSpec((tm, d_model), lambda i, j: (i, 0)),
        out_shape=jax.ShapeDtypeStruct((rows, d_model), F32),
        scratch_shapes=[pltpu.VMEM((tm, d_model), BF16), pltpu.VMEM((tm, d_model), F32)],
        compiler_params=_params("arbitrary", "arbitrary"),
        name="ffn_ln",
    )(x, wu_bf16, wd_bf16, g, b)


def _rope_tables(positions):
    half = D_HEAD // 2
    inv = 1.0 / (ROPE_THETA ** (jnp.arange(half, dtype=F32) / half))
    ang = positions.astype(F32)[:, None] * inv[None, :]
    cos, sin = jnp.cos(ang), jnp.sin(ang)
    return jnp.concatenate([cos, cos], axis=1), jnp.concatenate([-sin, sin], axis=1)


def _largest_tile(n, cap):
    t = min(n, cap)
    while n % t:
        t //= 2
    return t


def kernel(x_prompt, x_sample, cache_sb_k, cache_sb_v, cache_mb_k, cache_mb_v, cache_mem_k, cache_mem_v, state_pool, page_table, mem_prompt, w_in, w_mem_k, w_mem_v, pool_w, pool_scale, w_out, ln1_g, ln1_b, w_up, w_down, ln2_g, ln2_b):
    batch, seq, d_model = x_prompt.shape
    nseq, t_new, _ = x_sample.shape
    depth = w_in.shape[0]
    npages = page_table.shape[1]
    past = npages * PAGE_SIZE
    n_mem = mem_prompt.shape[1]
    rows_p, rows_s = batch * seq, nseq * t_new
    rows = rows_p + rows_s
    alpha = float((2 * depth) ** 0.25)
    hp = HEADS_PER_STEP
    assert d_model == 4 * GROUP_W and w_in.shape[2] == N_GROUPS_IN * GROUP_W
    assert seq % MOBA_BLOCK == 0 and past % MOBA_BLOCK == 0 and t_new <= PAGE_SIZE and t_new % 8 == 0
    assert state_pool.shape[2] == POOL_BUF and cache_sb_k.shape[2:] == (PAGE_SIZE, N_HEADS, D_HEAD)

    tm = _largest_tile(np.gcd(rows_p, rows_s), 512)
    n_phys = cache_sb_k.shape[1]
    paged = lambda c: c.reshape(depth, n_phys, PAGE_SIZE * N_HEADS, D_HEAD)
    csk, csv, cmk, cmv = paged(cache_sb_k), paged(cache_sb_v), paged(cache_mb_k), paged(cache_mb_v)
    memk = cache_mem_k.reshape(depth, nseq, n_mem * N_HEADS, D_HEAD)
    memv = cache_mem_v.reshape(depth, nseq, n_mem * N_HEADS, D_HEAD)
    mem2d = mem_prompt.reshape(batch * n_mem, d_model)

    pos = jnp.concatenate([jnp.tile(jnp.arange(seq, dtype=jnp.int32), batch),
                           jnp.tile(past + jnp.arange(t_new, dtype=jnp.int32), nseq)])
    cos_tab, sin_tab = _rope_tables(pos)

    x = jnp.concatenate([x_prompt.reshape(rows_p, d_model), x_sample.reshape(rows_s, d_model)], axis=0)
    states = None
    mem_states, pool_p, pool_s = [], [], []
    for l in range(depth):
        (u, q_sb, q_mb, q_mem), (k_sb, v_sb, k_mb, v_mb), kmean, states = _in_projection(
            l, depth, x, w_in[l].astype(BF16), cos_tab, sin_tab, tm, rows_p, states)
        pk_sb, pv_sb, pk_mb, pv_mb, sk_sb, sv_sb, sk_mb, sv_mb = states
        pw = pool_w[l].astype(BF16)
        psc = pool_scale[l].reshape(1, GROUP_W)
        u_s = u[rows_p:].reshape(nseq, t_new, GROUP_W)
        u_ext = jnp.concatenate([jnp.zeros((nseq, POOL_HALO - POOL_BUF, GROUP_W), F32), state_pool[l], u_s], axis=1)
        y_pool = _pool_prompt(u, pw, psc, batch, seq, _largest_tile(seq, 512))
        y_pool = _pool_sample(u_ext, pw, psc, y_pool, past, t_new, rows_p)
        y_sb = _sb_prompt(q_sb, k_sb, v_sb, batch, seq, 256, hp)
        y_sb = _paged_sample_call(_sb_sample_body, "sb_sample", l, page_table, q_sb, sk_sb, sv_sb,
                                  csk, csv, y_sb, rows_p, nseq, t_new)
        y_mb = _moba_prompt(q_mb, k_mb, v_mb, kmean, batch, seq, hp)
        y_mb = _paged_sample_call(_moba_sample_body, "moba_sample", l, page_table, q_mb, sk_mb, sv_mb,
                                  cmk, cmv, y_mb, rows_p, nseq, t_new)
        mk = _matmul(mem2d, w_mem_k[l].astype(BF16))
        mv = _matmul(mem2d, w_mem_v[l].astype(BF16))
        y_mem = _mem_prompt(q_mem, mk, mv, batch, seq, _largest_tile(seq, 512))
        y_mem = _mem_sample(l, q_mem, memk, memv, y_mem, rows_p, nseq, t_new, _largest_tile(nseq, 8))
        x = _finish_attn((y_pool, y_sb, y_mb, y_mem), x, w_out[l].astype(BF16),
                         ln1_g[l].reshape(1, d_model), ln1_b[l].reshape(1, d_model), alpha, tm)
        x = _ffn(x, w_up[l].astype(BF16), w_down[l].astype(BF16),
                 ln2_g[l].reshape(1, d_model), ln2_b[l].reshape(1, d_model), alpha, tm,
                 _largest_tile(w_up.shape[2], 1024))
        mem_states.append((mk.reshape(batch, n_mem, N_HEADS, D_HEAD), mv.reshape(batch, n_mem, N_HEADS, D_HEAD)))
        pool_p.append(u[:rows_p].reshape(batch, seq, GROUP_W)[:, seq - POOL_BUF:])
        pool_s.append(u_ext[:, u_ext.shape[1] - POOL_BUF:])

    heads_p = lambda a: a.reshape(depth, batch, seq, N_HEADS, D_HEAD)
    heads_s = lambda a: a.reshape(depth, nseq, t_new, N_HEADS, D_HEAD)
    return (x[:rows_p].reshape(batch, seq, d_model), x[rows_p:].reshape(nseq, t_new, d_model),
            heads_p(pk_sb), heads_p(pv_sb), heads_p(pk_mb), heads_p(pv_mb),
            jnp.stack([m[0] for m in mem_states]), jnp.stack([m[1] for m in mem_states]), jnp.stack(pool_p),
            heads_s(sk_sb), heads_s(sv_sb), heads_s(sk_mb), heads_s(sv_mb), jnp.stack(pool_s))
```

```python
import functools

import numpy as np
import jax
import jax.numpy as jnp
from jax import lax
from jax.experimental import pallas as pl
from jax.experimental.pallas import tpu as pltpu

F32 = jnp.float32
BF16 = jnp.bfloat16

D_HEAD = 128
N_HEADS = 4
GROUP_W = N_HEADS * D_HEAD
N_GROUPS_IN = 8
POOL_WINDOWS = (2, 4, 8, 16)
POOL_BUF = 15
POOL_HALO = 16
MOBA_BLOCK = 256
MOBA_TOPK = 3
PAGE_SIZE = 128
ROPE_THETA = 10000.0
LN_EPS = 1e-5
NEG_INF = -1e30
ATT_SCALE = D_HEAD ** -0.5
HEADS_PER_STEP = 2
SB_DEAD_LOG = -120.0
SB_HEAD_PAGES = 2

VMEM_LIMIT_BYTES = 56 * 1024 * 1024


def _params(*sem):
    return pltpu.CompilerParams(dimension_semantics=sem, vmem_limit_bytes=VMEM_LIMIT_BYTES)


def _dot(a, b):
    return jnp.dot(a, b, preferred_element_type=F32)


def _dot_nt(a, b):
    return lax.dot_general(a, b, (((1,), (1,)), ((), ())), preferred_element_type=F32)


def _split_hi_lo(x):
    hi = x.astype(BF16)
    lo = (x - hi.astype(F32)).astype(BF16)
    return hi, lo


def _softplus(z):
    return jnp.maximum(z, 0.0) + jnp.log(1.0 + jnp.exp(-jnp.abs(z)))


def _layer_norm(x, g, b):
    mu = jnp.mean(x, axis=-1, keepdims=True)
    xc = x - mu
    var = jnp.mean(xc * xc, axis=-1, keepdims=True)
    return xc * lax.rsqrt(var + LN_EPS) * g + b


def _head_cols(x, h):
    return x[:, h * D_HEAD:(h + 1) * D_HEAD]


def _load_heads(ref, tokens, lead=()):
    return jnp.concatenate(
        [ref[lead + (pl.ds(h, tokens, stride=N_HEADS), slice(None))] for h in range(N_HEADS)], axis=1)


def _store_heads(ref, val):
    for h in range(N_HEADS):
        ref[pl.ds(h, val.shape[0], stride=N_HEADS), :] = _head_cols(val, h)


def _rope(h, cos, sin):
    return jnp.concatenate(
        [_head_cols(h, c) * cos + pltpu.roll(_head_cols(h, c), D_HEAD // 2, 1) * sin
         for c in range(N_HEADS)], axis=1)


def _inproj_body(x_ref, w_ref, cos_ref, sin_ref, *refs, nbp, tm):
    (u_ref, qsb_ref, qmb_ref, qmem_ref, ksb_ref, vsb_ref, kmb_ref, vmb_ref, kmean_ref,
     pksb_ref, pvsb_ref, pkmb_ref, pvmb_ref, sksb_ref, svsb_ref, skmb_ref, svmb_ref, xb_ref) = refs[-18:]
    i = pl.program_id(0)
    xb_ref[...] = x_ref[...].astype(BF16)

    def group(g):
        return _dot(xb_ref[...], w_ref[:, g * GROUP_W:(g + 1) * GROUP_W])

    def keep_kv(val, bf_ref, p_ref, s_ref):
        bf_ref[...] = val.astype(BF16)
        pl.when(i < nbp)(lambda: _store_heads(p_ref, val))
        pl.when(i >= nbp)(lambda: _store_heads(s_ref, val))

    u_ref[...] = group(0)
    qsb_ref[...] = group(1)
    keep_kv(group(2), ksb_ref, pksb_ref, sksb_ref)
    keep_kv(group(3), vsb_ref, pvsb_ref, svsb_ref)
    qmb_ref[...] = _rope(group(4), cos_ref[...], sin_ref[...])
    r = _rope(group(5), cos_ref[...], sin_ref[...])
    keep_kv(r, kmb_ref, pkmb_ref, skmb_ref)
    for n in range(tm // MOBA_BLOCK):
        blk = r[n * MOBA_BLOCK:(n + 1) * MOBA_BLOCK, :]
        kmean_ref[n:n + 1, :] = jnp.sum(blk, axis=0, keepdims=True) * (1.0 / MOBA_BLOCK)
    keep_kv(group(6), vmb_ref, pvmb_ref, svmb_ref)
    qmem_ref[...] = group(7)


def _in_projection(layer, depth, x, w_bf16, cos_tab, sin_tab, tm, rows_p, prev_states):
    rows, d_model = x.shape
    rows_s = rows - rows_p
    nbp = rows_p // tm
    assert w_bf16.shape == (depth, d_model, N_GROUPS_IN * GROUP_W)
    assert rows_p % tm == 0 and rows_s % tm == 0 and tm % MOBA_BLOCK == 0
    row_spec = pl.BlockSpec((tm, GROUP_W), lambda i: (i, 0))
    p_spec = pl.BlockSpec((None, tm * N_HEADS, D_HEAD), lambda i: (layer, jnp.minimum(i, nbp - 1), 0))
    s_spec = pl.BlockSpec((None, tm * N_HEADS, D_HEAD), lambda i: (layer, jnp.maximum(i - nbp, 0), 0))
    nkm = tm // MOBA_BLOCK
    f32_sds = jax.ShapeDtypeStruct((rows, GROUP_W), F32)
    bf_sds = jax.ShapeDtypeStruct((rows, GROUP_W), BF16)
    p_sds = jax.ShapeDtypeStruct((depth, rows_p * N_HEADS, D_HEAD), F32)
    s_sds = jax.ShapeDtypeStruct((depth, rows_s * N_HEADS, D_HEAD), F32)
    in_specs = [
        pl.BlockSpec((tm, d_model), lambda i: (i, 0)),
        pl.BlockSpec((None, d_model, N_GROUPS_IN * GROUP_W), lambda i: (layer, 0, 0)),
        pl.BlockSpec((tm, D_HEAD), lambda i: (i, 0)),
        pl.BlockSpec((tm, D_HEAD), lambda i: (i, 0)),
    ]
    args = [x, w_bf16, cos_tab, sin_tab]
    aliases = {}
    if prev_states is not None:
        in_specs += [pl.BlockSpec(memory_space=pl.ANY)] * 8
        args += list(prev_states)
        aliases = {4 + k: 9 + k for k in range(8)}
    outs = pl.pallas_call(
        functools.partial(_inproj_body, nbp=nbp, tm=tm),
        grid=(rows // tm,),
        in_specs=in_specs,
        out_specs=[row_spec] * 8 + [pl.BlockSpec((None, nkm, GROUP_W), lambda i: (i, 0, 0))]
        + [p_spec] * 4 + [s_spec] * 4,
        out_shape=[f32_sds] * 4 + [bf_sds] * 4
        + [jax.ShapeDtypeStruct((rows // tm, nkm, GROUP_W), F32)] + [p_sds] * 4 + [s_sds] * 4,
        scratch_shapes=[pltpu.VMEM((tm, d_model), BF16)],
        input_output_aliases=aliases,
        compiler_params=_params("arbitrary"),
        name="in_projection",
    )(*args)
    return outs[0:4], outs[4:8], outs[8].reshape(rows // MOBA_BLOCK, GROUP_W), outs[9:17]


def _mm_body(x_ref, w_ref, o_ref):
    o_ref[...] = _dot(x_ref[...].astype(BF16), w_ref[...])


def _matmul(layer, x, w_bf16):
    m, k = x.shape
    n = w_bf16.shape[2]
    return pl.pallas_call(
        _mm_body,
        grid=(1,),
        in_specs=[pl.BlockSpec((m, k), lambda i: (0, 0)), pl.BlockSpec((None, k, n), lambda i: (layer, 0, 0))],
        out_specs=pl.BlockSpec((m, n), lambda i: (0, 0)),
        out_shape=jax.ShapeDtypeStruct((m, n), F32),
        compiler_params=_params("arbitrary"),
        name="mem_projection",
    )(x, w_bf16)


def _pool_finish(sums, u_groups, counts, pw_ref, sc_ref):
    ys = []
    for g in range(len(POOL_WINDOWS)):
        mixed = sums[g] / counts[g] - u_groups[g]
        ys.append(_dot(mixed.astype(BF16), pw_ref[g]))
    return jnp.concatenate(ys, axis=1) * sc_ref[...]


def _pool_prompt_body(u_ref, pw_ref, sc_ref, o_ref, ext_ref, *, tt):
    i = pl.program_id(1)
    gd = GROUP_W // len(POOL_WINDOWS)

    @pl.when(i == 0)
    def _reset():
        ext_ref[0:POOL_HALO, :] = jnp.zeros((POOL_HALO, GROUP_W), F32)

    ext_ref[POOL_HALO:POOL_HALO + tt, :] = u_ref[...]
    sums = [None] * len(POOL_WINDOWS)
    for s in range(max(POOL_WINDOWS)):
        g0 = min(g for g, w in enumerate(POOL_WINDOWS) if s < w)
        sh = ext_ref[pl.ds(POOL_HALO - s, tt), pl.ds(g0 * gd, GROUP_W - g0 * gd)]
        for g in range(g0, len(POOL_WINDOWS)):
            piece = sh[:, (g - g0) * gd:(g - g0 + 1) * gd]
            sums[g] = piece if sums[g] is None else sums[g] + piece
    pos = i * tt + lax.broadcasted_iota(jnp.int32, (tt, gd), 0)
    counts = [jnp.minimum(w, pos + 1).astype(F32) for w in POOL_WINDOWS]
    u = u_ref[...]
    u_groups = [u[:, g * gd:(g + 1) * gd] for g in range(len(POOL_WINDOWS))]
    o_ref[...] = _pool_finish(sums, u_groups, counts, pw_ref, sc_ref)
    ext_ref[0:POOL_HALO, :] = ext_ref[tt:tt + POOL_HALO, :]


def _pool_prompt(layer, u_all, pw_bf16, scale, batch, seq, tt):
    nt = seq // tt
    return pl.pallas_call(
        functools.partial(_pool_prompt_body, tt=tt),
        grid=(batch, nt),
        in_specs=[
            pl.BlockSpec((tt, GROUP_W), lambda b, i: (b * nt + i, 0)),
            pl.BlockSpec((None,) + pw_bf16.shape[1:], lambda b, i: (layer, 0, 0, 0)),
            pl.BlockSpec((None, 1, GROUP_W), lambda b, i: (layer, 0, 0)),
        ],
        out_specs=pl.BlockSpec((tt, GROUP_W), lambda b, i: (b * nt + i, 0)),
        out_shape=jax.ShapeDtypeStruct(u_all.shape, F32),
        scratch_shapes=[pltpu.VMEM((POOL_HALO + tt, GROUP_W), F32)],
        compiler_params=_params("arbitrary", "arbitrary"),
        name="pool_prompt",
    )(u_all, pw_bf16, scale)


def _pool_sample_body(ue_ref, pw_ref, sc_ref, y_hbm, o_ref, *, past, t_new):
    del y_hbm
    nseq = ue_ref.shape[0]
    gd = GROUP_W // len(POOL_WINDOWS)
    sums = [None] * len(POOL_WINDOWS)
    for s in range(max(POOL_WINDOWS)):
        g0 = min(g for g, w in enumerate(POOL_WINDOWS) if s < w)
        sh = ue_ref[:, pl.ds(POOL_HALO - s, t_new), pl.ds(g0 * gd, GROUP_W - g0 * gd)]
        for g in range(g0, len(POOL_WINDOWS)):
            piece = sh[:, :, (g - g0) * gd:(g - g0 + 1) * gd].reshape(nseq * t_new, gd)
            sums[g] = piece if sums[g] is None else sums[g] + piece
    pos = past + lax.broadcasted_iota(jnp.int32, (nseq, t_new, gd), 1).reshape(nseq * t_new, gd)
    counts = [jnp.minimum(w, pos + 1).astype(F32) for w in POOL_WINDOWS]
    u = ue_ref[:, pl.ds(POOL_HALO, t_new), :]
    u_groups = [u[:, :, g * gd:(g + 1) * gd].reshape(nseq * t_new, gd) for g in range(len(POOL_WINDOWS))]
    o_ref[...] = _pool_finish(sums, u_groups, counts, pw_ref, sc_ref)


def _pool_sample(layer, u_ext, pw_bf16, scale, y_all, past, t_new, rows_p):
    nseq = u_ext.shape[0]
    rows_s = nseq * t_new
    assert rows_p % rows_s == 0
    return pl.pallas_call(
        functools.partial(_pool_sample_body, past=past, t_new=t_new),
        grid=(1,),
        in_specs=[
            pl.BlockSpec(u_ext.shape, lambda i: (0, 0, 0)),
            pl.BlockSpec((None,) + pw_bf16.shape[1:], lambda i: (layer, 0, 0, 0)),
            pl.BlockSpec((None, 1, GROUP_W), lambda i: (layer, 0, 0)),
            pl.BlockSpec(memory_space=pl.ANY),
        ],
        out_specs=pl.BlockSpec((rows_s, GROUP_W), lambda i: (rows_p // rows_s, 0)),
        out_shape=jax.ShapeDtypeStruct(y_all.shape, F32),
        input_output_aliases={3: 0},
        compiler_params=_params("arbitrary"),
        name="pool_sample",
    )(u_ext, pw_bf16, scale, y_all)


def _suffix_sum_matrix(n):
    row = lax.broadcasted_iota(jnp.int32, (n, n), 0)
    col = lax.broadcasted_iota(jnp.int32, (n, n), 1)
    upper = jnp.where(row > col, 1.0, 0.0).astype(BF16)
    return jnp.concatenate([upper, jnp.ones((n, n), BF16)], axis=1)


def _sb_blocks(qs, ks, vs, u2, carries, valid):
    n = ks[0].shape[0]
    zs = [_dot_nt(q, k) * ATT_SCALE for q, k in zip(qs, ks)]
    sps = [_softplus(z) for z in zs]
    log_fails = [-sp if valid is None else jnp.where(valid, -sp, 0.0) for sp in sps]
    csts = [_dot(lf.astype(BF16), u2) for lf in log_fails]
    ws = [jnp.exp(z - sp + cst[:, :n] + c) for z, sp, cst, c in zip(zs, sps, csts, carries)]
    if valid is not None:
        ws = [jnp.where(valid, w, 0.0) for w in ws]
    pvs = [_dot(w.astype(BF16), v) for w, v in zip(ws, vs)]
    return pvs, [c + cst[:, n:] for c, cst in zip(carries, csts)]


def _sb_prompt_body(q_ref, k_ref, v_ref, o_ref, acc_ref, car_ref, live_ref, *, tq, hp):
    qi = pl.program_id(2)
    qs = [_head_cols(q_ref, h).astype(BF16) for h in range(hp)]
    u2 = _suffix_sum_matrix(tq)
    row = lax.broadcasted_iota(jnp.int32, (tq, tq), 0)
    col = lax.broadcasted_iota(jnp.int32, (tq, tq), 1)

    def blocks(kb, carries, valid):
        off = pl.multiple_of(kb * tq, tq)
        ks = [k_ref[pl.ds(off, tq), pl.ds(h * D_HEAD, D_HEAD)] for h in range(hp)]
        vs = [v_ref[pl.ds(off, tq), pl.ds(h * D_HEAD, D_HEAD)] for h in range(hp)]
        return _sb_blocks(qs, ks, vs, u2, carries, valid)

    def keep(pvs, cars, first):
        live = None
        for h in range(hp):
            acc_ref[h] = pvs[h] if first else acc_ref[h] + pvs[h]
            car_ref[h] = cars[h]
            top = jnp.max(cars[h][:, :D_HEAD])
            live = top if live is None else jnp.maximum(live, top)
        live_ref[0] = (live > SB_DEAD_LOG).astype(jnp.int32)

    keep(*blocks(qi, [jnp.zeros((tq, tq), F32)] * hp, col < row), True)

    def past(t):
        keep(*blocks(qi - 1 - t, [car_ref[h] for h in range(hp)], None), False)
        return t + 1

    lax.while_loop(lambda t: (t < qi) & (live_ref[0] > 0), past, 0)
    o_ref[...] = jnp.concatenate([acc_ref[h] for h in range(hp)], axis=1)


def _sb_prompt(q_all, k_bf, v_bf, batch, seq, tq, hp):
    nq = seq // tq
    wide = hp * D_HEAD
    kv_spec = pl.BlockSpec((seq, wide), lambda b, h, i: (b, h))
    return pl.pallas_call(
        functools.partial(_sb_prompt_body, tq=tq, hp=hp),
        grid=(batch, N_HEADS // hp, nq),
        in_specs=[pl.BlockSpec((tq, wide), lambda b, h, i: (b * nq + i, h)), kv_spec, kv_spec],
        out_specs=pl.BlockSpec((tq, wide), lambda b, h, i: (b * nq + i, h)),
        out_shape=jax.ShapeDtypeStruct(q_all.shape, F32),
        scratch_shapes=[pltpu.VMEM((hp, tq, D_HEAD), F32), pltpu.VMEM((hp, tq, tq), F32),
                        pltpu.SMEM((1,), jnp.int32)],
        compiler_params=_params("arbitrary", "arbitrary", "arbitrary"),
        name="sb_prompt",
    )(q_all, k_bf, v_bf)


def _head_masks(t_new):
    rows = N_HEADS * t_new
    r = lax.broadcasted_iota(jnp.int32, (rows, GROUP_W), 0)
    c = lax.broadcasted_iota(jnp.int32, (rows, GROUP_W), 1)
    same = None
    for h in range(N_HEADS):
        m = (r >= h * t_new) & (r < (h + 1) * t_new) & (c >= h * D_HEAD) & (c < (h + 1) * D_HEAD)
        same = m if same is None else same | m
    return same


def _stack_heads(q, t_new):
    qrep = jnp.concatenate([q] * N_HEADS, axis=0)
    return jnp.where(_head_masks(t_new), qrep, 0.0)


def _unstack_heads(acc, t_new):
    masked = jnp.where(_head_masks(t_new), acc, 0.0)
    out = masked[0:t_new]
    for h in range(1, N_HEADS):
        out = out + masked[h * t_new:(h + 1) * t_new]
    return out


def _row_time(t_new, width):
    r = lax.broadcasted_iota(jnp.int32, (N_HEADS * t_new, width), 0)
    t = r
    for h in range(1, N_HEADS):
        t = jnp.where(r >= h * t_new, r - h * t_new, t)
    return t


def _pad_rows(x, rows):
    return jnp.concatenate([x, jnp.zeros((rows - x.shape[0], x.shape[1]), x.dtype)], axis=0)


def _sb_sample_blocks(qb, ks, vs, u2, carry, valid_first):
    rows = qb.shape[0]
    zs = [_dot_nt(qb, k) * ATT_SCALE for k in ks]
    sps = [_softplus(z) for z in zs]
    log_fails = [-sp for sp in sps]
    if valid_first is not None:
        log_fails[0] = jnp.where(valid_first, log_fails[0], 0.0)
    cst = _dot(jnp.concatenate(log_fails, axis=0).astype(BF16), u2)
    acc = None
    for b in range(len(ks)):
        cb = cst[b * rows:(b + 1) * rows]
        w = jnp.exp(zs[b] - sps[b] + cb[:, :PAGE_SIZE] + carry)
        carry = carry + cb[:, PAGE_SIZE:]
        if b == 0 and valid_first is not None:
            w = jnp.where(valid_first, w, 0.0)
        pv = _dot(w.astype(BF16), vs[b]())
        acc = pv if acc is None else acc + pv
    return acc, carry


def _sb_sample_head_body(pt_ref, q_ref, kn_ref, vn_ref, *refs, nhead, t_new):
    kp, vp = refs[:nhead], refs[nhead:2 * nhead]
    acc_ref, car_ref, live_ref = refs[2 * nhead:]
    rows = N_HEADS * t_new
    qb = _stack_heads(q_ref[...], t_new).astype(BF16)
    col = lax.broadcasted_iota(jnp.int32, (rows, PAGE_SIZE), 1)
    ks = [_pad_rows(_load_heads(kn_ref, t_new), PAGE_SIZE).astype(BF16)]
    ks += [_load_heads(kp[p], PAGE_SIZE).astype(BF16) for p in range(nhead - 1, -1, -1)]
    vs = [lambda: _pad_rows(_load_heads(vn_ref, t_new), PAGE_SIZE).astype(BF16)]
    vs += [functools.partial(lambda p: _load_heads(vp[p], PAGE_SIZE).astype(BF16), p)
           for p in range(nhead - 1, -1, -1)]
    acc, carry = _sb_sample_blocks(qb, ks, vs, _suffix_sum_matrix(PAGE_SIZE),
                                   jnp.zeros((rows, PAGE_SIZE), F32), col < _row_time(t_new, PAGE_SIZE))
    acc_ref[...] = acc
    car_ref[...] = carry
    top = jnp.max(carry, axis=0, keepdims=True)
    live_ref[...] = jnp.broadcast_to(top > SB_DEAD_LOG, live_ref.shape).astype(jnp.int32)


def _sb_sample_tail_body(pt_ref, live_ref, q_ref, acc_ref, car_ref, *refs, ntail, t_new):
    kp, vp, o_ref = refs[:ntail], refs[ntail:2 * ntail], refs[-1]
    b = pl.program_id(0)

    @pl.when(live_ref[b] == 0)
    def _done():
        o_ref[...] = _unstack_heads(acc_ref[...], t_new)

    @pl.when(live_ref[b] != 0)
    def _more():
        qb = _stack_heads(q_ref[...], t_new).astype(BF16)
        ks = [_load_heads(kp[p], PAGE_SIZE).astype(BF16) for p in range(ntail - 1, -1, -1)]
        vs = [functools.partial(lambda p: _load_heads(vp[p], PAGE_SIZE).astype(BF16), p)
              for p in range(ntail - 1, -1, -1)]
        acc, _ = _sb_sample_blocks(qb, ks, vs, _suffix_sum_matrix(PAGE_SIZE), car_ref[...], None)
        o_ref[...] = _unstack_heads(acc_ref[...] + acc, t_new)


PAGE_BLOCK = (None, None, PAGE_SIZE * N_HEADS, D_HEAD)


def _paged_specs(layer, npages, pages):
    def spec(p):
        return pl.BlockSpec(PAGE_BLOCK, lambda b, pt: (layer, pt[b * npages + p], 0, 0))
    return [spec(p) for p in pages]


def _sb_sample(layer, page_table, q_all, k_new, v_new, cache_k, cache_v, y_all, rows_p, nseq, t_new):
    npages = page_table.shape[1]
    nhead = min(SB_HEAD_PAGES, npages - 1)
    ntail = npages - nhead
    rows = N_HEADS * t_new
    blk0 = rows_p // t_new
    pt_flat = page_table.reshape(-1)
    new_spec = pl.BlockSpec((None, t_new * N_HEADS, D_HEAD), lambda b, pt: (layer, b, 0))
    head_pages = list(range(ntail, npages))
    acc, carry, live = pl.pallas_call(
        functools.partial(_sb_sample_head_body, nhead=nhead, t_new=t_new),
        grid_spec=pltpu.PrefetchScalarGridSpec(
            num_scalar_prefetch=1,
            grid=(nseq,),
            in_specs=[pl.BlockSpec((t_new, GROUP_W), lambda b, pt: (blk0 + b, 0)), new_spec, new_spec]
            + _paged_specs(layer, npages, head_pages) + _paged_specs(layer, npages, head_pages),
            out_specs=[pl.BlockSpec((rows, GROUP_W), lambda b, pt: (b, 0)),
                       pl.BlockSpec((rows, PAGE_SIZE), lambda b, pt: (b, 0)),
                       pl.BlockSpec((None, 8, PAGE_SIZE), lambda b, pt: (b, 0, 0))],
        ),
        out_shape=[jax.ShapeDtypeStruct((nseq * rows, GROUP_W), F32),
                   jax.ShapeDtypeStruct((nseq * rows, PAGE_SIZE), F32),
                   jax.ShapeDtypeStruct((nseq, 8, PAGE_SIZE), jnp.int32)],
        compiler_params=_params("arbitrary"),
        name="sb_sample_head",
    )(pt_flat, q_all, k_new, v_new, *([cache_k] * nhead), *([cache_v] * nhead))

    def tail_spec(p):
        return pl.BlockSpec(PAGE_BLOCK, lambda b, pt, lv: (
            layer, jnp.where(lv[b] != 0, pt[b * npages + p], 0), 0, 0))

    tail_specs = [tail_spec(p) for p in range(ntail)]
    return pl.pallas_call(
        functools.partial(_sb_sample_tail_body, ntail=ntail, t_new=t_new),
        grid_spec=pltpu.PrefetchScalarGridSpec(
            num_scalar_prefetch=2,
            grid=(nseq,),
            in_specs=[pl.BlockSpec((t_new, GROUP_W), lambda b, pt, lv: (blk0 + b, 0)),
                      pl.BlockSpec((rows, GROUP_W), lambda b, pt, lv: (b, 0)),
                      pl.BlockSpec((rows, PAGE_SIZE), lambda b, pt, lv: (b, 0))]
            + tail_specs + tail_specs + [pl.BlockSpec(memory_space=pl.ANY)],
            out_specs=pl.BlockSpec((t_new, GROUP_W), lambda b, pt, lv: (blk0 + b, 0)),
        ),
        out_shape=jax.ShapeDtypeStruct(y_all.shape, F32),
        input_output_aliases={5 + 2 * ntail: 0},
        compiler_params=_params("arbitrary"),
        name="sb_sample_tail",
    )(pt_flat, live[:, 0, 0], q_all, acc, carry, *([cache_k] * ntail), *([cache_v] * ntail), y_all)


def _paged_sample_call(body, name, layer, page_table, q_all, k_new, v_new, cache_k, cache_v, y_all,
                       rows_p, nseq, t_new):
    npages = page_table.shape[1]
    blk0 = rows_p // t_new
    q_spec = pl.BlockSpec((t_new, GROUP_W), lambda b, pt: (blk0 + b, 0))
    new_spec = pl.BlockSpec((None, t_new * N_HEADS, D_HEAD), lambda b, pt: (layer, b, 0))
    grid_spec = pltpu.PrefetchScalarGridSpec(
        num_scalar_prefetch=1,
        grid=(nseq,),
        in_specs=[q_spec, new_spec, new_spec] + 2 * _paged_specs(layer, npages, range(npages))
        + [pl.BlockSpec(memory_space=pl.ANY)],
        out_specs=pl.BlockSpec((t_new, GROUP_W), lambda b, pt: (blk0 + b, 0)),
    )
    n_in = 4 + 2 * npages
    return pl.pallas_call(
        functools.partial(body, npages=npages, t_new=t_new),
        grid_spec=grid_spec,
        out_shape=jax.ShapeDtypeStruct(y_all.shape, F32),
        input_output_aliases={n_in: 0},
        compiler_params=_params("arbitrary"),
        name=name,
    )(page_table.reshape(-1), q_all, k_new, v_new, *([cache_k] * npages), *([cache_v] * npages), y_all)


def _moba_prompt_body(q_ref, k_ref, v_ref, km_ref, o_ref, sel_ref, *, tq, nb, hp):
    qi = pl.program_id(2)
    lane = lax.broadcasted_iota(jnp.int32, (tq, D_HEAD), 1)
    past_blk = lane < qi
    qbs = []
    for h in range(hp):
        q = _head_cols(q_ref, h)
        qbs.append(q.astype(BF16))
        q_hi, q_lo = _split_hi_lo(q)
        km_hi, km_lo = _split_hi_lo(_pad_rows(_head_cols(km_ref, h), D_HEAD))
        gate = _dot_nt(q_hi, km_hi) + _dot_nt(q_lo, km_hi) + _dot_nt(q_hi, km_lo)
        gate = jnp.where(past_blk, gate, NEG_INF)
        rank = jnp.zeros(gate.shape, jnp.int32)
        for m in range(nb):
            gm = gate[:, m:m + 1]
            beats = (gm > gate) | ((gm == gate) & (m < lane))
            rank = rank + beats.astype(jnp.int32)
        sel_ref[h] = jnp.where(past_blk & (rank < MOBA_TOPK), 1.0, 0.0)

    row = lax.broadcasted_iota(jnp.int32, (tq, tq), 0)
    col = lax.broadcasted_iota(jnp.int32, (tq, tq), 1)

    def scores(n):
        off = pl.multiple_of(n * tq, tq)
        ss = [_dot_nt(qbs[h], k_ref[pl.ds(off, tq), pl.ds(h * D_HEAD, D_HEAD)]) * ATT_SCALE for h in range(hp)]
        vs = [v_ref[pl.ds(off, tq), pl.ds(h * D_HEAD, D_HEAD)] for h in range(hp)]
        return ss, vs

    ss, vs = scores(qi)
    ss = [jnp.where(col <= row, s, NEG_INF) for s in ss]
    ms = [jnp.max(s, axis=1, keepdims=True) for s in ss]
    ps = [jnp.exp(s - m) for s, m in zip(ss, ms)]
    ls = [jnp.sum(p, axis=1, keepdims=True) for p in ps]
    accs = [_dot(p.astype(BF16), v) for p, v in zip(ps, vs)]

    def past(n, carry):
        ms, ls, accs = carry
        ss, vs = scores(n)
        chosen = [jnp.max(jnp.where(lane == n, sel_ref[h], 0.0), axis=1, keepdims=True) > 0.0 for h in range(hp)]
        ss = [jnp.where(c, s, NEG_INF) for c, s in zip(chosen, ss)]
        m_new = [jnp.maximum(m, jnp.max(s, axis=1, keepdims=True)) for m, s in zip(ms, ss)]
        alphas = [jnp.exp(m - mn) for m, mn in zip(ms, m_new)]
        ps = [jnp.exp(s - mn) for s, mn in zip(ss, m_new)]
        ls = [a * l + jnp.sum(p, axis=1, keepdims=True) for a, l, p in zip(alphas, ls, ps)]
        accs = [a * acc + _dot(p.astype(BF16), v) for a, acc, p, v in zip(alphas, accs, ps, vs)]
        return m_new, ls, accs

    _, ls, accs = lax.fori_loop(0, qi, past, (ms, ls, accs))
    o_ref[...] = jnp.concatenate([acc / l for acc, l in zip(accs, ls)], axis=1)


def _moba_prompt(q_all, k_bf, v_bf, kmean, batch, seq, hp):
    tq = MOBA_BLOCK
    nq = seq // tq
    wide = hp * D_HEAD
    kv_spec = pl.BlockSpec((seq, wide), lambda b, h, i: (b, h))
    return pl.pallas_call(
        functools.partial(_moba_prompt_body, tq=tq, nb=nq, hp=hp),
        grid=(batch, N_HEADS // hp, nq),
        in_specs=[pl.BlockSpec((tq, wide), lambda b, h, i: (b * nq + i, h)), kv_spec, kv_spec,
                  pl.BlockSpec((nq, wide), lambda b, h, i: (b, h))],
        out_specs=pl.BlockSpec((tq, wide), lambda b, h, i: (b * nq + i, h)),
        out_shape=jax.ShapeDtypeStruct(q_all.shape, F32),
        scratch_shapes=[pltpu.VMEM((hp, tq, D_HEAD), F32)],
        compiler_params=_params("arbitrary", "arbitrary", "arbitrary"),
        name="moba_prompt",
    )(q_all, k_bf, v_bf, kmean)


def _moba_sample_body(pt_ref, q_ref, kn_ref, vn_ref, *refs, npages, t_new):
    kp, vp, o_ref = refs[:npages], refs[npages:2 * npages], refs[-1]
    rows = N_HEADS * t_new
    ppb = MOBA_BLOCK // PAGE_SIZE
    nblk = npages // ppb
    q_st = _stack_heads(q_ref[...], t_new)
    qb = q_st.astype(BF16)

    k_pages = [_load_heads(kp[p], PAGE_SIZE) for p in range(npages)]
    gates = []
    for n in range(nblk):
        ksum = k_pages[n * ppb].sum(axis=0, keepdims=True)
        for j in range(1, ppb):
            ksum = ksum + k_pages[n * ppb + j].sum(axis=0, keepdims=True)
        gates.append(jnp.sum(q_st * (ksum * (1.0 / MOBA_BLOCK)), axis=1, keepdims=True))
    chosen = []
    for n in range(nblk):
        rank = jnp.zeros((rows, 1), jnp.int32)
        for m in range(nblk):
            if m != n:
                beats = (gates[m] > gates[n]) | ((gates[m] == gates[n]) & (m < n))
                rank = rank + beats.astype(jnp.int32)
        chosen.append(rank < MOBA_TOPK)

    col = lax.broadcasted_iota(jnp.int32, (rows, PAGE_SIZE), 1)
    kn = _pad_rows(_load_heads(kn_ref, t_new), PAGE_SIZE).astype(BF16)
    s_own = jnp.where(col <= _row_time(t_new, PAGE_SIZE), _dot_nt(qb, kn) * ATT_SCALE, NEG_INF)
    s_pages = []
    for p in range(npages):
        s = _dot_nt(qb, k_pages[p].astype(BF16)) * ATT_SCALE
        s_pages.append(jnp.where(chosen[p // ppb], s, NEG_INF))
    m = jnp.max(s_own, axis=1, keepdims=True)
    for s in s_pages:
        m = jnp.maximum(m, jnp.max(s, axis=1, keepdims=True))
    p_own = jnp.exp(s_own - m)
    l = jnp.sum(p_own, axis=1, keepdims=True)
    acc = _dot(p_own.astype(BF16), _pad_rows(_load_heads(vn_ref, t_new), PAGE_SIZE).astype(BF16))
    for p in range(npages):
        pr = jnp.exp(s_pages[p] - m)
        l = l + jnp.sum(pr, axis=1, keepdims=True)
        acc = acc + _dot(pr.astype(BF16), _load_heads(vp[p], PAGE_SIZE).astype(BF16))
    o_ref[...] = _unstack_heads(acc / l, t_new)


def _softmax_pv(s, v):
    m = jnp.max(s, axis=1, keepdims=True)
    p = jnp.exp(s - m)
    l = jnp.sum(p, axis=1, keepdims=True)
    return _dot(p.astype(BF16), v) / l


def _mem_prompt_body(q_ref, mk_ref, mv_ref, o_ref):
    s = _dot_nt(q_ref[...].astype(BF16), mk_ref[...].astype(BF16)) * ATT_SCALE
    o_ref[...] = _softmax_pv(s, mv_ref[...].astype(BF16))


def _mem_prompt(q_all, mk, mv, batch, seq, tq):
    nq = seq // tq
    n_mem = mk.shape[0] // batch
    kv_spec = pl.BlockSpec((n_mem, D_HEAD), lambda b, h, i: (b, h))
    return pl.pallas_call(
        _mem_prompt_body,
        grid=(batch, N_HEADS, nq),
        in_specs=[pl.BlockSpec((tq, D_HEAD), lambda b, h, i: (b * nq + i, h)), kv_spec, kv_spec],
        out_specs=pl.BlockSpec((tq, D_HEAD), lambda b, h, i: (b * nq + i, h)),
        out_shape=jax.ShapeDtypeStruct(q_all.shape, F32),
        compiler_params=_params("arbitrary", "arbitrary", "arbitrary"),
        name="mem_prompt",
    )(q_all, mk, mv)


def _mem_sample_body(q_ref, mk_ref, mv_ref, y_hbm, o_ref, *, nb, t_new, n_mem):
    del y_hbm
    for i in range(nb):
        qb = _stack_heads(q_ref[i * t_new:(i + 1) * t_new, :], t_new).astype(BF16)
        s = _dot_nt(qb, _load_heads(mk_ref, n_mem, (i,)).astype(BF16)) * ATT_SCALE
        pv = _softmax_pv(s, _load_heads(mv_ref, n_mem, (i,)).astype(BF16))
        o_ref[i * t_new:(i + 1) * t_new, :] = _unstack_heads(pv, t_new)


def _mem_sample(layer, q_all, mem_k, mem_v, y_all, rows_p, nseq, t_new, nb):
    n_mem = mem_k.shape[2] // N_HEADS
    blk0 = rows_p // (nb * t_new)
    kv_spec = pl.BlockSpec((None, nb, n_mem * N_HEADS, D_HEAD), lambda i: (layer, i, 0, 0))
    return pl.pallas_call(
        functools.partial(_mem_sample_body, nb=nb, t_new=t_new, n_mem=n_mem),
        grid=(nseq // nb,),
        in_specs=[pl.BlockSpec((nb * t_new, GROUP_W), lambda i: (blk0 + i, 0)), kv_spec, kv_spec,
                  pl.BlockSpec(memory_space=pl.ANY)],
        out_specs=pl.BlockSpec((nb * t_new, GROUP_W), lambda i: (blk0 + i, 0)),
        out_shape=jax.ShapeDtypeStruct(y_all.shape, F32),
        input_output_aliases={3: 0},
        compiler_params=_params("arbitrary"),
        name="mem_sample",
    )(q_all, mem_k, mem_v, y_all)


def _finish_attn_body(y0, y1, y2, y3, x_ref, w_ref, g_ref, b_ref, o_ref, *, alpha):
    acc = alpha * x_ref[...]
    for gi, y_ref in enumerate((y0, y1, y2, y3)):
        acc = acc + _dot(y_ref[...].astype(BF16), w_ref[gi * GROUP_W:(gi + 1) * GROUP_W, :])
    o_ref[...] = _layer_norm(acc, g_ref[...], b_ref[...])


def _finish_attn(layer, parts, x, w_bf16, g, b, alpha, tm):
    rows, d_model = x.shape
    part_spec = pl.BlockSpec((tm, GROUP_W), lambda i: (i, 0))
    vec_spec = pl.BlockSpec((None, 1, d_model), lambda i: (layer, 0, 0))
    return pl.pallas_call(
        functools.partial(_finish_attn_body, alpha=alpha),
        grid=(rows // tm,),
        in_specs=[part_spec] * 4 + [pl.BlockSpec((tm, d_model), lambda i: (i, 0)),
                                    pl.BlockSpec((None,) + w_bf16.shape[1:], lambda i: (layer, 0, 0)),
                                    vec_spec, vec_spec],
        out_specs=pl.BlockSpec((tm, d_model), lambda i: (i, 0)),
        out_shape=jax.ShapeDtypeStruct((rows, d_model), F32),
        compiler_params=_params("arbitrary"),
        name="out_projection_ln",
    )(*parts, x, w_bf16, g, b)


def _ffn_body(x_ref, wu_ref, wd_ref, g_ref, b_ref, o_ref, xb_ref, acc_ref, *, alpha):
    j = pl.program_id(1)

    @pl.when(j == 0)
    def _init():
        xb_ref[...] = x_ref[...].astype(BF16)
        acc_ref[...] = jnp.zeros(acc_ref.shape, F32)

    hid = jnp.square(jnp.maximum(_dot(xb_ref[...], wu_ref[...]), 0.0))
    acc_ref[...] += _dot(hid.astype(BF16), wd_ref[...])

    @pl.when(j == pl.num_programs(1) - 1)
    def _finish():
        o_ref[...] = _layer_norm(alpha * x_ref[...] + acc_ref[...], g_ref[...], b_ref[...])


def _ffn(layer, x, wu_bf16, wd_bf16, g, b, alpha, tm, tf):
    rows, d_model = x.shape
    d_ff = wu_bf16.shape[2]
    vec_spec = pl.BlockSpec((None, 1, d_model), lambda i, j: (layer, 0, 0))
    return pl.pallas_call(
        functools.partial(_ffn_body, alpha=alpha),
        grid=(rows // tm, d_ff // tf),
        in_specs=[pl.BlockSpec((tm, d_model), lambda i, j: (i, 0)),
                  pl.BlockSpec((None, d_model, tf), lambda i, j: (layer, 0, j)),
                  pl.BlockSpec((None, tf, d_model), lambda i, j: (layer, j, 0)), vec_spec, vec_spec],
        out_specs=pl.BlockSpec((tm, d_model), lambda i, j: (i, 0)),
        out_shape=jax.ShapeDtypeStruct((rows, d_model), F32),
        scratch_shapes=[pltpu.VMEM((tm, d_model), BF16), pltpu.VMEM((tm, d_model), F32)],
        compiler_params=_params("arbitrary", "arbitrary"),
        name="ffn_ln",
    )(x, wu_bf16, wd_bf16, g, b)


def _rope_tables(positions):
    half = D_HEAD // 2
    inv = 1.0 / (ROPE_THETA ** (jnp.arange(half, dtype=F32) / half))
    ang = positions.astype(F32)[:, None] * inv[None, :]
    cos, sin = jnp.cos(ang), jnp.sin(ang)
    return jnp.concatenate([cos, cos], axis=1), jnp.concatenate([-sin, sin], axis=1)


def _largest_tile(n, cap):
    t = min(n, cap)
    while n % t:
        t //= 2
    return t


def kernel(x_prompt, x_sample, cache_sb_k, cache_sb_v, cache_mb_k, cache_mb_v, cache_mem_k, cache_mem_v, state_pool, page_table, mem_prompt, w_in, w_mem_k, w_mem_v, pool_w, pool_scale, w_out, ln1_g, ln1_b, w_up, w_down, ln2_g, ln2_b):
    batch, seq, d_model = x_prompt.shape
    nseq, t_new, _ = x_sample.shape
    depth = w_in.shape[0]
    npages = page_table.shape[1]
    past = npages * PAGE_SIZE
    n_mem = mem_prompt.shape[1]
    rows_p, rows_s = batch * seq, nseq * t_new
    rows = rows_p + rows_s
    alpha = float((2 * depth) ** 0.25)
    hp = HEADS_PER_STEP
    assert d_model == 4 * GROUP_W and w_in.shape[2] == N_GROUPS_IN * GROUP_W
    assert seq % MOBA_BLOCK == 0 and past % MOBA_BLOCK == 0 and t_new <= PAGE_SIZE and t_new % 8 == 0
    assert state_pool.shape[2] == POOL_BUF and cache_sb_k.shape[2:] == (PAGE_SIZE, N_HEADS, D_HEAD)

    tm = _largest_tile(np.gcd(rows_p, rows_s), 512)
    n_phys = cache_sb_k.shape[1]
    paged = lambda c: c.reshape(depth, n_phys, PAGE_SIZE * N_HEADS, D_HEAD)
    csk, csv, cmk, cmv = paged(cache_sb_k), paged(cache_sb_v), paged(cache_mb_k), paged(cache_mb_v)
    memk = cache_mem_k.reshape(depth, nseq, n_mem * N_HEADS, D_HEAD)
    memv = cache_mem_v.reshape(depth, nseq, n_mem * N_HEADS, D_HEAD)
    mem2d = mem_prompt.reshape(batch * n_mem, d_model)

    pos = jnp.concatenate([jnp.tile(jnp.arange(seq, dtype=jnp.int32), batch),
                           jnp.tile(past + jnp.arange(t_new, dtype=jnp.int32), nseq)])
    cos_tab, sin_tab = _rope_tables(pos)

    w_in_b, w_out_b, w_up_b, w_down_b = (w.astype(BF16) for w in (w_in, w_out, w_up, w_down))
    w_mem_k_b, w_mem_v_b, pool_w_b = w_mem_k.astype(BF16), w_mem_v.astype(BF16), pool_w.astype(BF16)
    vec = lambda v: v.reshape(depth, 1, v.shape[1])
    pool_sc, g1, b1, g2, b2 = vec(pool_scale), vec(ln1_g), vec(ln1_b), vec(ln2_g), vec(ln2_b)
    tm_in = _largest_tile(np.gcd(rows_p, rows_s), MOBA_BLOCK)
    tf = _largest_tile(w_up.shape[2], 1024)

    x = jnp.concatenate([x_prompt.reshape(rows_p, d_model), x_sample.reshape(rows_s, d_model)], axis=0)
    states = None
    mem_states, pool_p, pool_s = [], [], []
    for l in range(depth):
        (u, q_sb, q_mb, q_mem), (k_sb, v_sb, k_mb, v_mb), kmean, states = _in_projection(
            l, depth, x, w_in_b, cos_tab, sin_tab, tm_in, rows_p, states)
        pk_sb, pv_sb, pk_mb, pv_mb, sk_sb, sv_sb, sk_mb, sv_mb = states
        u_s = u[rows_p:].reshape(nseq, t_new, GROUP_W)
        u_ext = jnp.concatenate([jnp.zeros((nseq, POOL_HALO - POOL_BUF, GROUP_W), F32), state_pool[l], u_s], axis=1)
        y_pool = _pool_prompt(l, u, pool_w_b, pool_sc, batch, seq, _largest_tile(seq, 512))
        y_pool = _pool_sample(l, u_ext, pool_w_b, pool_sc, y_pool, past, t_new, rows_p)
        y_sb = _sb_prompt(q_sb, k_sb, v_sb, batch, seq, 256, hp)
        y_sb = _sb_sample(l, page_table, q_sb, sk_sb, sv_sb, csk, csv, y_sb, rows_p, nseq, t_new)
        y_mb = _moba_prompt(q_mb, k_mb, v_mb, kmean, batch, seq, hp)
        y_mb = _paged_sample_call(_moba_sample_body, "moba_sample", l, page_table, q_mb, sk_mb, sv_mb,
                                  cmk, cmv, y_mb, rows_p, nseq, t_new)
        mk = _matmul(l, mem2d, w_mem_k_b)
        mv = _matmul(l, mem2d, w_mem_v_b)
        y_mem = _mem_prompt(q_mem, mk, mv, batch, seq, _largest_tile(seq, 512))
        y_mem = _mem_sample(l, q_mem, memk, memv, y_mem, rows_p, nseq, t_new, _largest_tile(nseq, 8))
        x = _finish_attn(l, (y_pool, y_sb, y_mb, y_mem), x, w_out_b, g1, b1, alpha, tm)
        x = _ffn(l, x, w_up_b, w_down_b, g2, b2, alpha, tm, tf)
        mem_states.append((mk.reshape(batch, n_mem, N_HEADS, D_HEAD), mv.reshape(batch, n_mem, N_HEADS, D_HEAD)))
        pool_p.append(u[:rows_p].reshape(batch, seq, GROUP_W)[:, seq - POOL_BUF:])
        pool_s.append(u_ext[:, u_ext.shape[1] - POOL_BUF:])

    heads_p = lambda a: a.reshape(depth, batch, seq, N_HEADS, D_HEAD)
    heads_s = lambda a: a.reshape(depth, nseq, t_new, N_HEADS, D_HEAD)
    return (x[:rows_p].reshape(batch, seq, d_model), x[rows_p:].reshape(nseq, t_new, d_model),
            heads_p(pk_sb), heads_p(pv_sb), heads_p(pk_mb), heads_p(pv_mb),
            jnp.stack([m[0] for m in mem_states]), jnp.stack([m[1] for m in mem_states]), jnp.stack(pool_p),
            heads_s(sk_sb), heads_s(sv_sb), heads_s(sk_mb), heads_s(sv_mb), jnp.stack(pool_s))
```

```python
import functools

import numpy as np
import jax
import jax.numpy as jnp
from jax import lax
from jax.experimental import pallas as pl
from jax.experimental.pallas import tpu as pltpu

F32 = jnp.float32
BF16 = jnp.bfloat16

D_HEAD = 128
N_HEADS = 4
GROUP_W = N_HEADS * D_HEAD
N_GROUPS_IN = 8
POOL_WINDOWS = (2, 4, 8, 16)
POOL_BUF = 15
POOL_HALO = 16
MOBA_BLOCK = 256
MOBA_TOPK = 3
PAGE_SIZE = 128
ROPE_THETA = 10000.0
LN_EPS = 1e-5
NEG_INF = -1e30
ATT_SCALE = D_HEAD ** -0.5
SB_HEADS_PER_STEP = 2
MOBA_HEADS_PER_STEP = 4
SB_DEAD_LOG = -120.0
SB_HEAD_PAGES = 2

VMEM_LIMIT_BYTES = 56 * 1024 * 1024


def _params(*sem):
    return pltpu.CompilerParams(dimension_semantics=sem, vmem_limit_bytes=VMEM_LIMIT_BYTES)


def _dot(a, b):
    return jnp.dot(a, b, preferred_element_type=F32)


def _dot_nt(a, b):
    return lax.dot_general(a, b, (((1,), (1,)), ((), ())), preferred_element_type=F32)


def _split_hi_lo(x):
    hi = x.astype(BF16)
    lo = (x - hi.astype(F32)).astype(BF16)
    return hi, lo


def _softplus(z):
    return jnp.maximum(z, 0.0) + jnp.log(1.0 + jnp.exp(-jnp.abs(z)))


def _layer_norm(x, g, b):
    mu = jnp.mean(x, axis=-1, keepdims=True)
    xc = x - mu
    var = jnp.mean(xc * xc, axis=-1, keepdims=True)
    return xc * lax.rsqrt(var + LN_EPS) * g + b


def _head_cols(x, h):
    return x[:, h * D_HEAD:(h + 1) * D_HEAD]


def _load_heads(ref, tokens, lead=()):
    return jnp.concatenate(
        [ref[lead + (pl.ds(h, tokens, stride=N_HEADS), slice(None))] for h in range(N_HEADS)], axis=1)


def _store_heads(ref, val):
    for h in range(N_HEADS):
        ref[pl.ds(h, val.shape[0], stride=N_HEADS), :] = _head_cols(val, h)


def _rope(h, cos, sin):
    return jnp.concatenate(
        [_head_cols(h, c) * cos + pltpu.roll(_head_cols(h, c), D_HEAD // 2, 1) * sin
         for c in range(N_HEADS)], axis=1)


def _inproj_body(x_ref, w_ref, cos_ref, sin_ref, *refs, nbp, tm):
    (u_ref, qsb_ref, qmb_ref, qmem_ref, ksb_ref, vsb_ref, kmb_ref, vmb_ref, kmean_ref,
     pksb_ref, pvsb_ref, pkmb_ref, pvmb_ref, sksb_ref, svsb_ref, skmb_ref, svmb_ref, xb_ref) = refs[-18:]
    i = pl.program_id(0)
    xb_ref[...] = x_ref[...].astype(BF16)

    def group(g):
        return _dot(xb_ref[...], w_ref[:, g * GROUP_W:(g + 1) * GROUP_W])

    def keep_kv(val, bf_ref, p_ref, s_ref):
        bf_ref[...] = val.astype(BF16)
        pl.when(i < nbp)(lambda: _store_heads(p_ref, val))
        pl.when(i >= nbp)(lambda: _store_heads(s_ref, val))

    u_ref[...] = group(0)
    qsb_ref[...] = group(1)
    keep_kv(group(2), ksb_ref, pksb_ref, sksb_ref)
    keep_kv(group(3), vsb_ref, pvsb_ref, svsb_ref)
    qmb_ref[...] = _rope(group(4), cos_ref[...], sin_ref[...])
    r = _rope(group(5), cos_ref[...], sin_ref[...])
    keep_kv(r, kmb_ref, pkmb_ref, skmb_ref)
    for n in range(tm // MOBA_BLOCK):
        blk = r[n * MOBA_BLOCK:(n + 1) * MOBA_BLOCK, :]
        kmean_ref[n:n + 1, :] = jnp.sum(blk, axis=0, keepdims=True) * (1.0 / MOBA_BLOCK)
    keep_kv(group(6), vmb_ref, pvmb_ref, svmb_ref)
    qmem_ref[...] = group(7)


def _in_projection(layer, depth, x, w_bf16, cos_tab, sin_tab, tm, rows_p, prev_states):
    rows, d_model = x.shape
    rows_s = rows - rows_p
    nbp = rows_p // tm
    assert w_bf16.shape == (depth, d_model, N_GROUPS_IN * GROUP_W)
    assert rows_p % tm == 0 and rows_s % tm == 0 and tm % MOBA_BLOCK == 0
    row_spec = pl.BlockSpec((tm, GROUP_W), lambda i: (i, 0))
    p_spec = pl.BlockSpec((None, tm * N_HEADS, D_HEAD), lambda i: (layer, jnp.minimum(i, nbp - 1), 0))
    s_spec = pl.BlockSpec((None, tm * N_HEADS, D_HEAD), lambda i: (layer, jnp.maximum(i - nbp, 0), 0))
    nkm = tm // MOBA_BLOCK
    f32_sds = jax.ShapeDtypeStruct((rows, GROUP_W), F32)
    bf_sds = jax.ShapeDtypeStruct((rows, GROUP_W), BF16)
    p_sds = jax.ShapeDtypeStruct((depth, rows_p * N_HEADS, D_HEAD), F32)
    s_sds = jax.ShapeDtypeStruct((depth, rows_s * N_HEADS, D_HEAD), F32)
    in_specs = [
        pl.BlockSpec((tm, d_model), lambda i: (i, 0)),
        pl.BlockSpec((None, d_model, N_GROUPS_IN * GROUP_W), lambda i: (layer, 0, 0)),
        pl.BlockSpec((tm, D_HEAD), lambda i: (i, 0)),
        pl.BlockSpec((tm, D_HEAD), lambda i: (i, 0)),
    ]
    args = [x, w_bf16, cos_tab, sin_tab]
    aliases = {}
    if prev_states is not None:
        in_specs += [pl.BlockSpec(memory_space=pl.ANY)] * 8
        args += list(prev_states)
        aliases = {4 + k: 9 + k for k in range(8)}
    outs = pl.pallas_call(
        functools.partial(_inproj_body, nbp=nbp, tm=tm),
        grid=(rows // tm,),
        in_specs=in_specs,
        out_specs=[row_spec] * 8 + [pl.BlockSpec((None, nkm, GROUP_W), lambda i: (i, 0, 0))]
        + [p_spec] * 4 + [s_spec] * 4,
        out_shape=[f32_sds] * 4 + [bf_sds] * 4
        + [jax.ShapeDtypeStruct((rows // tm, nkm, GROUP_W), F32)] + [p_sds] * 4 + [s_sds] * 4,
        scratch_shapes=[pltpu.VMEM((tm, d_model), BF16)],
        input_output_aliases=aliases,
        compiler_params=_params("arbitrary"),
        name="in_projection",
    )(*args)
    return outs[0:4], outs[4:8], outs[8].reshape(rows // MOBA_BLOCK, GROUP_W), outs[9:17]


def _mm_body(x_ref, w_ref, o_ref):
    o_ref[...] = _dot(x_ref[...].astype(BF16), w_ref[...])


def _matmul(layer, x, w_bf16):
    m, k = x.shape
    n = w_bf16.shape[2]
    return pl.pallas_call(
        _mm_body,
        grid=(1,),
        in_specs=[pl.BlockSpec((m, k), lambda i: (0, 0)), pl.BlockSpec((None, k, n), lambda i: (layer, 0, 0))],
        out_specs=pl.BlockSpec((m, n), lambda i: (0, 0)),
        out_shape=jax.ShapeDtypeStruct((m, n), F32),
        compiler_params=_params("arbitrary"),
        name="mem_projection",
    )(x, w_bf16)


def _pool_finish(sums, u_groups, counts, pw_ref, sc_ref):
    ys = []
    for g in range(len(POOL_WINDOWS)):
        mixed = sums[g] / counts[g] - u_groups[g]
        ys.append(_dot(mixed.astype(BF16), pw_ref[g]))
    return jnp.concatenate(ys, axis=1) * sc_ref[...]


def _pool_prompt_body(u_ref, pw_ref, sc_ref, o_ref, ext_ref, *, tt):
    i = pl.program_id(1)
    gd = GROUP_W // len(POOL_WINDOWS)

    @pl.when(i == 0)
    def _reset():
        ext_ref[0:POOL_HALO, :] = jnp.zeros((POOL_HALO, GROUP_W), F32)

    ext_ref[POOL_HALO:POOL_HALO + tt, :] = u_ref[...]
    sums = [None] * len(POOL_WINDOWS)
    for s in range(max(POOL_WINDOWS)):
        g0 = min(g for g, w in enumerate(POOL_WINDOWS) if s < w)
        sh = ext_ref[pl.ds(POOL_HALO - s, tt), pl.ds(g0 * gd, GROUP_W - g0 * gd)]
        for g in range(g0, len(POOL_WINDOWS)):
            piece = sh[:, (g - g0) * gd:(g - g0 + 1) * gd]
            sums[g] = piece if sums[g] is None else sums[g] + piece
    pos = i * tt + lax.broadcasted_iota(jnp.int32, (tt, gd), 0)
    counts = [jnp.minimum(w, pos + 1).astype(F32) for w in POOL_WINDOWS]
    u = u_ref[...]
    u_groups = [u[:, g * gd:(g + 1) * gd] for g in range(len(POOL_WINDOWS))]
    o_ref[...] = _pool_finish(sums, u_groups, counts, pw_ref, sc_ref)
    ext_ref[0:POOL_HALO, :] = ext_ref[tt:tt + POOL_HALO, :]


def _pool_prompt(layer, u_all, pw_bf16, scale, batch, seq, tt):
    nt = seq // tt
    return pl.pallas_call(
        functools.partial(_pool_prompt_body, tt=tt),
        grid=(batch, nt),
        in_specs=[
            pl.BlockSpec((tt, GROUP_W), lambda b, i: (b * nt + i, 0)),
            pl.BlockSpec((None,) + pw_bf16.shape[1:], lambda b, i: (layer, 0, 0, 0)),
            pl.BlockSpec((None, 1, GROUP_W), lambda b, i: (layer, 0, 0)),
        ],
        out_specs=pl.BlockSpec((tt, GROUP_W), lambda b, i: (b * nt + i, 0)),
        out_shape=jax.ShapeDtypeStruct(u_all.shape, F32),
        scratch_shapes=[pltpu.VMEM((POOL_HALO + tt, GROUP_W), F32)],
        compiler_params=_params("arbitrary", "arbitrary"),
        name="pool_prompt",
    )(u_all, pw_bf16, scale)


def _pool_sample_body(ue_ref, pw_ref, sc_ref, y_hbm, o_ref, *, past, t_new):
    del y_hbm
    nseq = ue_ref.shape[0]
    gd = GROUP_W // len(POOL_WINDOWS)
    sums = [None] * len(POOL_WINDOWS)
    for s in range(max(POOL_WINDOWS)):
        g0 = min(g for g, w in enumerate(POOL_WINDOWS) if s < w)
        sh = ue_ref[:, pl.ds(POOL_HALO - s, t_new), pl.ds(g0 * gd, GROUP_W - g0 * gd)]
        for g in range(g0, len(POOL_WINDOWS)):
            piece = sh[:, :, (g - g0) * gd:(g - g0 + 1) * gd].reshape(nseq * t_new, gd)
            sums[g] = piece if sums[g] is None else sums[g] + piece
    pos = past + lax.broadcasted_iota(jnp.int32, (nseq, t_new, gd), 1).reshape(nseq * t_new, gd)
    counts = [jnp.minimum(w, pos + 1).astype(F32) for w in POOL_WINDOWS]
    u = ue_ref[:, pl.ds(POOL_HALO, t_new), :]
    u_groups = [u[:, :, g * gd:(g + 1) * gd].reshape(nseq * t_new, gd) for g in range(len(POOL_WINDOWS))]
    o_ref[...] = _pool_finish(sums, u_groups, counts, pw_ref, sc_ref)


def _pool_sample(layer, u_ext, pw_bf16, scale, y_all, past, t_new, rows_p):
    nseq = u_ext.shape[0]
    rows_s = nseq * t_new
    assert rows_p % rows_s == 0
    return pl.pallas_call(
        functools.partial(_pool_sample_body, past=past, t_new=t_new),
        grid=(1,),
        in_specs=[
            pl.BlockSpec(u_ext.shape, lambda i: (0, 0, 0)),
            pl.BlockSpec((None,) + pw_bf16.shape[1:], lambda i: (layer, 0, 0, 0)),
            pl.BlockSpec((None, 1, GROUP_W), lambda i: (layer, 0, 0)),
            pl.BlockSpec(memory_space=pl.ANY),
        ],
        out_specs=pl.BlockSpec((rows_s, GROUP_W), lambda i: (rows_p // rows_s, 0)),
        out_shape=jax.ShapeDtypeStruct(y_all.shape, F32),
        input_output_aliases={3: 0},
        compiler_params=_params("arbitrary"),
        name="pool_sample",
    )(u_ext, pw_bf16, scale, y_all)


def _suffix_sum_matrix(n):
    row = lax.broadcasted_iota(jnp.int32, (n, n), 0)
    col = lax.broadcasted_iota(jnp.int32, (n, n), 1)
    upper = jnp.where(row > col, 1.0, 0.0).astype(BF16)
    return jnp.concatenate([upper, jnp.ones((n, n), BF16)], axis=1)


def _sb_blocks(qs, ks, vs, u2, carries, valid):
    n = ks[0].shape[0]
    zs = [_dot_nt(q, k) * ATT_SCALE for q, k in zip(qs, ks)]
    sps = [_softplus(z) for z in zs]
    log_fails = [-sp if valid is None else jnp.where(valid, -sp, 0.0) for sp in sps]
    csts = [_dot(lf.astype(BF16), u2) for lf in log_fails]
    ws = [jnp.exp(z - sp + cst[:, :n] + c) for z, sp, cst, c in zip(zs, sps, csts, carries)]
    if valid is not None:
        ws = [jnp.where(valid, w, 0.0) for w in ws]
    pvs = [_dot(w.astype(BF16), v) for w, v in zip(ws, vs)]
    return pvs, [c + cst[:, n:] for c, cst in zip(carries, csts)]


def _sb_prompt_body(q_ref, k_ref, v_ref, o_ref, acc_ref, car_ref, live_ref, *, tq, hp):
    qi = pl.program_id(2)
    qs = [_head_cols(q_ref, h).astype(BF16) for h in range(hp)]
    u2 = _suffix_sum_matrix(tq)
    row = lax.broadcasted_iota(jnp.int32, (tq, tq), 0)
    col = lax.broadcasted_iota(jnp.int32, (tq, tq), 1)

    def blocks(kb, carries, valid):
        off = pl.multiple_of(kb * tq, tq)
        ks = [k_ref[pl.ds(off, tq), pl.ds(h * D_HEAD, D_HEAD)] for h in range(hp)]
        vs = [v_ref[pl.ds(off, tq), pl.ds(h * D_HEAD, D_HEAD)] for h in range(hp)]
        return _sb_blocks(qs, ks, vs, u2, carries, valid)

    def keep(pvs, cars, first):
        live = None
        for h in range(hp):
            acc_ref[h] = pvs[h] if first else acc_ref[h] + pvs[h]
            car_ref[h] = cars[h]
            top = jnp.max(cars[h][:, :D_HEAD])
            live = top if live is None else jnp.maximum(live, top)
        live_ref[0] = (live > SB_DEAD_LOG).astype(jnp.int32)

    keep(*blocks(qi, [jnp.zeros((tq, tq), F32)] * hp, col < row), True)

    def past(t):
        keep(*blocks(qi - 1 - t, [car_ref[h] for h in range(hp)], None), False)
        return t + 1

    lax.while_loop(lambda t: (t < qi) & (live_ref[0] > 0), past, 0)
    o_ref[...] = jnp.concatenate([acc_ref[h] for h in range(hp)], axis=1)


def _sb_prompt(q_all, k_bf, v_bf, batch, seq, tq, hp):
    nq = seq // tq
    wide = hp * D_HEAD
    kv_spec = pl.BlockSpec((seq, wide), lambda b, h, i: (b, h))
    return pl.pallas_call(
        functools.partial(_sb_prompt_body, tq=tq, hp=hp),
        grid=(batch, N_HEADS // hp, nq),
        in_specs=[pl.BlockSpec((tq, wide), lambda b, h, i: (b * nq + i, h)), kv_spec, kv_spec],
        out_specs=pl.BlockSpec((tq, wide), lambda b, h, i: (b * nq + i, h)),
        out_shape=jax.ShapeDtypeStruct(q_all.shape, F32),
        scratch_shapes=[pltpu.VMEM((hp, tq, D_HEAD), F32), pltpu.VMEM((hp, tq, tq), F32),
                        pltpu.SMEM((1,), jnp.int32)],
        compiler_params=_params("arbitrary", "arbitrary", "arbitrary"),
        name="sb_prompt",
    )(q_all, k_bf, v_bf)


def _head_masks(t_new):
    rows = N_HEADS * t_new
    r = lax.broadcasted_iota(jnp.int32, (rows, GROUP_W), 0)
    c = lax.broadcasted_iota(jnp.int32, (rows, GROUP_W), 1)
    same = None
    for h in range(N_HEADS):
        m = (r >= h * t_new) & (r < (h + 1) * t_new) & (c >= h * D_HEAD) & (c < (h + 1) * D_HEAD)
        same = m if same is None else same | m
    return same


def _stack_heads(q, t_new):
    qrep = jnp.concatenate([q] * N_HEADS, axis=0)
    return jnp.where(_head_masks(t_new), qrep, 0.0)


def _unstack_heads(acc, t_new):
    masked = jnp.where(_head_masks(t_new), acc, 0.0)
    out = masked[0:t_new]
    for h in range(1, N_HEADS):
        out = out + masked[h * t_new:(h + 1) * t_new]
    return out


def _row_time(t_new, width):
    r = lax.broadcasted_iota(jnp.int32, (N_HEADS * t_new, width), 0)
    t = r
    for h in range(1, N_HEADS):
        t = jnp.where(r >= h * t_new, r - h * t_new, t)
    return t


def _pad_rows(x, rows):
    if rows == x.shape[0]:
        return x
    return jnp.concatenate([x, jnp.zeros((rows - x.shape[0], x.shape[1]), x.dtype)], axis=0)


def _sb_sample_blocks(qb, ks, vs, u2, carry, valid_first):
    rows = qb.shape[0]
    zs = [_dot_nt(qb, k) * ATT_SCALE for k in ks]
    sps = [_softplus(z) for z in zs]
    log_fails = [-sp for sp in sps]
    if valid_first is not None:
        log_fails[0] = jnp.where(valid_first, log_fails[0], 0.0)
    cst = _dot(jnp.concatenate(log_fails, axis=0).astype(BF16), u2)
    acc = None
    for b in range(len(ks)):
        cb = cst[b * rows:(b + 1) * rows]
        w = jnp.exp(zs[b] - sps[b] + cb[:, :PAGE_SIZE] + carry)
        carry = carry + cb[:, PAGE_SIZE:]
        if b == 0 and valid_first is not None:
            w = jnp.where(valid_first, w, 0.0)
        pv = _dot(w.astype(BF16), vs[b]())
        acc = pv if acc is None else acc + pv
    return acc, carry


def _sb_sample_head(q_ref, kn_ref, vn_ref, kp, vp, y_ref, acc_ref, car_ref, live_ref, t_new):
    nhead = len(kp)
    rows = N_HEADS * t_new
    qb = _stack_heads(q_ref[...], t_new).astype(BF16)
    col = lax.broadcasted_iota(jnp.int32, (rows, PAGE_SIZE), 1)
    ks = [_pad_rows(_load_heads(kn_ref, t_new), PAGE_SIZE).astype(BF16)]
    ks += [_load_heads(kp[p], PAGE_SIZE).astype(BF16) for p in range(nhead - 1, -1, -1)]
    vs = [lambda: _pad_rows(_load_heads(vn_ref, t_new), PAGE_SIZE).astype(BF16)]
    vs += [functools.partial(lambda p: _load_heads(vp[p], PAGE_SIZE).astype(BF16), p)
           for p in range(nhead - 1, -1, -1)]
    acc, carry = _sb_sample_blocks(qb, ks, vs, _suffix_sum_matrix(PAGE_SIZE),
                                   jnp.zeros((rows, PAGE_SIZE), F32), col < _row_time(t_new, PAGE_SIZE))
    y_ref[...] = _unstack_heads(acc, t_new)
    acc_ref[...] = acc
    car_ref[...] = carry
    top = jnp.max(carry, axis=0, keepdims=True)
    live_ref[...] = jnp.broadcast_to(top > SB_DEAD_LOG, live_ref.shape).astype(jnp.int32)


def _sb_sample_tail_body(pt_ref, live_ref, q_ref, acc_ref, car_ref, yin_ref, *refs, ntail, t_new):
    kp, vp, o_ref = refs[:ntail], refs[ntail:2 * ntail], refs[-1]
    b = pl.program_id(0)

    @pl.when(live_ref[b] == 0)
    def _done():
        o_ref[...] = yin_ref[...]

    @pl.when(live_ref[b] != 0)
    def _more():
        qb = _stack_heads(q_ref[...], t_new).astype(BF16)
        ks = [_load_heads(kp[p], PAGE_SIZE).astype(BF16) for p in range(ntail - 1, -1, -1)]
        vs = [functools.partial(lambda p: _load_heads(vp[p], PAGE_SIZE).astype(BF16), p)
              for p in range(ntail - 1, -1, -1)]
        acc, _ = _sb_sample_blocks(qb, ks, vs, _suffix_sum_matrix(PAGE_SIZE), car_ref[...], None)
        o_ref[...] = _unstack_heads(acc_ref[...] + acc, t_new)


PAGE_BLOCK = (None, None, PAGE_SIZE * N_HEADS, D_HEAD)


def _paged_specs(layer, npages, pages):
    def spec(p):
        return pl.BlockSpec(PAGE_BLOCK, lambda b, pt: (layer, pt[b * npages + p], 0, 0))
    return [spec(p) for p in pages]


def _paged_sample_body(pt_ref, qsb_ref, ksbn_ref, vsbn_ref, qmb_ref, kmbn_ref, vmbn_ref, *refs,
                       nhead, npages, t_new):
    sbk, sbv = refs[:nhead], refs[nhead:2 * nhead]
    mbk, mbv = refs[2 * nhead:2 * nhead + npages], refs[2 * nhead + npages:2 * nhead + 2 * npages]
    ysb_ref, ymb_ref, acc_ref, car_ref, live_ref = refs[-5:]
    _sb_sample_head(qsb_ref, ksbn_ref, vsbn_ref, sbk, sbv, ysb_ref, acc_ref, car_ref, live_ref, t_new)
    _moba_sample(qmb_ref, kmbn_ref, vmbn_ref, mbk, mbv, ymb_ref, t_new)


def _paged_sample(layer, page_table, q_sb, sb_new_k, sb_new_v, q_mb, mb_new_k, mb_new_v,
                  cache_sb_k, cache_sb_v, cache_mb_k, cache_mb_v, y_sb, y_mb, rows_p, nseq, t_new):
    npages = page_table.shape[1]
    nhead = min(SB_HEAD_PAGES, npages - 1)
    ntail = npages - nhead
    rows = N_HEADS * t_new
    blk0 = rows_p // t_new
    pt_flat = page_table.reshape(-1)
    q_spec = pl.BlockSpec((t_new, GROUP_W), lambda b, pt: (blk0 + b, 0))
    new_spec = pl.BlockSpec((None, t_new * N_HEADS, D_HEAD), lambda b, pt: (layer, b, 0))
    head_pages = list(range(ntail, npages))
    n_in = 7 + 2 * nhead + 2 * npages
    y_sb, y_mb, acc, carry, live = pl.pallas_call(
        functools.partial(_paged_sample_body, nhead=nhead, npages=npages, t_new=t_new),
        grid_spec=pltpu.PrefetchScalarGridSpec(
            num_scalar_prefetch=1,
            grid=(nseq,),
            in_specs=[q_spec, new_spec, new_spec, q_spec, new_spec, new_spec]
            + 2 * _paged_specs(layer, npages, head_pages) + 2 * _paged_specs(layer, npages, range(npages))
            + [pl.BlockSpec(memory_space=pl.ANY)] * 2,
            out_specs=[q_spec, q_spec,
                       pl.BlockSpec((rows, GROUP_W), lambda b, pt: (b, 0)),
                       pl.BlockSpec((rows, PAGE_SIZE), lambda b, pt: (b, 0)),
                       pl.BlockSpec((None, 8, PAGE_SIZE), lambda b, pt: (b, 0, 0))],
        ),
        out_shape=[jax.ShapeDtypeStruct(y_sb.shape, F32), jax.ShapeDtypeStruct(y_mb.shape, F32),
                   jax.ShapeDtypeStruct((nseq * rows, GROUP_W), F32),
                   jax.ShapeDtypeStruct((nseq * rows, PAGE_SIZE), F32),
                   jax.ShapeDtypeStruct((nseq, 8, PAGE_SIZE), jnp.int32)],
        input_output_aliases={n_in: 0, n_in + 1: 1},
        compiler_params=_params("arbitrary"),
        name="paged_sample",
    )(pt_flat, q_sb, sb_new_k, sb_new_v, q_mb, mb_new_k, mb_new_v,
      *([cache_sb_k] * nhead), *([cache_sb_v] * nhead), *([cache_mb_k] * npages), *([cache_mb_v] * npages),
      y_sb, y_mb)
    live = live[:, 0, 0]

    def tail_spec(p):
        return pl.BlockSpec(PAGE_BLOCK, lambda b, pt, lv: (
            layer, jnp.where(lv[b] != 0, pt[b * npages + p], 0), 0, 0))

    tail_specs = [tail_spec(p) for p in range(ntail)]
    row_spec = pl.BlockSpec((t_new, GROUP_W), lambda b, pt, lv: (blk0 + b, 0))

    def tail(y):
        return pl.pallas_call(
            functools.partial(_sb_sample_tail_body, ntail=ntail, t_new=t_new),
            grid_spec=pltpu.PrefetchScalarGridSpec(
                num_scalar_prefetch=2,
                grid=(nseq,),
                in_specs=[row_spec,
                          pl.BlockSpec((rows, GROUP_W), lambda b, pt, lv: (b, 0)),
                          pl.BlockSpec((rows, PAGE_SIZE), lambda b, pt, lv: (b, 0)), row_spec]
                + tail_specs + tail_specs,
                out_specs=row_spec,
            ),
            out_shape=jax.ShapeDtypeStruct(y.shape, F32),
            input_output_aliases={5: 0},
            compiler_params=_params("arbitrary"),
            name="sb_sample_tail",
        )(pt_flat, live, q_sb, acc, carry, y, *([cache_sb_k] * ntail), *([cache_sb_v] * ntail))

    y_sb = lax.cond(jnp.any(live != 0), tail, lambda y: y, y_sb)
    return y_sb, y_mb


def _moba_prompt_body(q_ref, k_ref, v_ref, km_ref, o_ref, *, tq, nb, hp):
    qi = pl.program_id(2)
    nbp = -(-nb // 8) * 8
    blk = lax.broadcasted_iota(jnp.int32, (nbp, tq), 0)
    lane = lax.broadcasted_iota(jnp.int32, (tq, D_HEAD), 1)
    qbs, q_augs = [], []
    for h in range(hp):
        q = _head_cols(q_ref, h)
        q_hi, q_lo = _split_hi_lo(q)
        km_hi, km_lo = _split_hi_lo(_pad_rows(_head_cols(km_ref, h), nbp))
        gate = _dot_nt(km_hi, q_hi) + _dot_nt(km_hi, q_lo) + _dot_nt(km_lo, q_hi)
        gate = jnp.where(blk < qi, gate, NEG_INF)
        rank = jnp.zeros(gate.shape, jnp.int32)
        for m in range(nb):
            gm = gate[m:m + 1, :]
            beats = (gm > gate) | ((gm == gate) & (m < blk))
            rank = rank + beats.astype(jnp.int32)
        bias_t = jnp.where((blk < qi) & (rank < MOBA_TOPK), 0.0, NEG_INF)
        bias = _pad_rows(bias_t, D_HEAD).T
        qbs.append(q.astype(BF16))
        q_augs.append(jnp.concatenate([qbs[h], bias.astype(BF16)], axis=1))

    row = lax.broadcasted_iota(jnp.int32, (tq, tq), 0)
    col = lax.broadcasted_iota(jnp.int32, (tq, tq), 1)

    def kv(n, h):
        off = pl.multiple_of(n * tq, tq)
        cols = pl.ds(h * D_HEAD, D_HEAD)
        return k_ref[pl.ds(off, tq), cols], v_ref[pl.ds(off, tq), cols]

    kvs = [kv(qi, h) for h in range(hp)]
    ss = [jnp.where(col <= row, _dot_nt(qbs[h], kvs[h][0]) * ATT_SCALE, NEG_INF) for h in range(hp)]
    ms = [jnp.max(s, axis=1, keepdims=True) for s in ss]
    ps = [jnp.exp(s - m) for s, m in zip(ss, ms)]
    ls = [jnp.sum(p, axis=1, keepdims=True) for p in ps]
    accs = [_dot(p.astype(BF16), kvs[h][1]) for h, p in enumerate(ps)]

    def past(n, carry):
        ms, ls, accs = carry
        one_hot = jnp.where(lane == n, 1.0, 0.0).astype(BF16)
        kvs = [kv(n, h) for h in range(hp)]
        vs = [v for _, v in kvs]
        ss = [_dot_nt(q_augs[h], jnp.concatenate([kvs[h][0], one_hot], axis=1)) * ATT_SCALE for h in range(hp)]
        m_new = [jnp.maximum(m, jnp.max(s, axis=1, keepdims=True)) for m, s in zip(ms, ss)]
        alphas = [jnp.exp(m - mn) for m, mn in zip(ms, m_new)]
        ps = [jnp.exp(s - mn) for s, mn in zip(ss, m_new)]
        ls = [a * l + jnp.sum(p, axis=1, keepdims=True) for a, l, p in zip(alphas, ls, ps)]
        accs = [a * acc + _dot(p.astype(BF16), v) for a, acc, p, v in zip(alphas, accs, ps, vs)]
        return m_new, ls, accs

    _, ls, accs = lax.fori_loop(0, qi, past, (ms, ls, accs))
    o_ref[...] = jnp.concatenate([acc / l for acc, l in zip(accs, ls)], axis=1)


def _moba_prompt(q_all, k_bf, v_bf, kmean, batch, seq, hp):
    tq = MOBA_BLOCK
    nq = seq // tq
    wide = hp * D_HEAD
    kv_spec = pl.BlockSpec((seq, wide), lambda b, h, i: (b, h))
    return pl.pallas_call(
        functools.partial(_moba_prompt_body, tq=tq, nb=nq, hp=hp),
        grid=(batch, N_HEADS // hp, nq),
        in_specs=[pl.BlockSpec((tq, wide), lambda b, h, i: (b * nq + i, h)), kv_spec, kv_spec,
                  pl.BlockSpec((nq, wide), lambda b, h, i: (b, h))],
        out_specs=pl.BlockSpec((tq, wide), lambda b, h, i: (b * nq + i, h)),
        out_shape=jax.ShapeDtypeStruct(q_all.shape, F32),
        compiler_params=_params("arbitrary", "arbitrary", "arbitrary"),
        name="moba_prompt",
    )(q_all, k_bf, v_bf, kmean)


def _moba_sample(q_ref, kn_ref, vn_ref, kp, vp, o_ref, t_new):
    npages = len(kp)
    rows = N_HEADS * t_new
    ppb = MOBA_BLOCK // PAGE_SIZE
    nblk = npages // ppb
    q_st = _stack_heads(q_ref[...], t_new)
    qb = q_st.astype(BF16)

    k_pages = [_load_heads(kp[p], PAGE_SIZE) for p in range(npages)]
    gates = []
    for n in range(nblk):
        ksum = k_pages[n * ppb].sum(axis=0, keepdims=True)
        for j in range(1, ppb):
            ksum = ksum + k_pages[n * ppb + j].sum(axis=0, keepdims=True)
        gates.append(jnp.sum(q_st * (ksum * (1.0 / MOBA_BLOCK)), axis=1, keepdims=True))
    chosen = []
    for n in range(nblk):
        rank = jnp.zeros((rows, 1), jnp.int32)
        for m in range(nblk):
            if m != n:
                beats = (gates[m] > gates[n]) | ((gates[m] == gates[n]) & (m < n))
                rank = rank + beats.astype(jnp.int32)
        chosen.append(rank < MOBA_TOPK)

    col = lax.broadcasted_iota(jnp.int32, (rows, PAGE_SIZE), 1)
    kn = _pad_rows(_load_heads(kn_ref, t_new), PAGE_SIZE).astype(BF16)
    s_own = jnp.where(col <= _row_time(t_new, PAGE_SIZE), _dot_nt(qb, kn) * ATT_SCALE, NEG_INF)
    s_pages = []
    for p in range(npages):
        s = _dot_nt(qb, k_pages[p].astype(BF16)) * ATT_SCALE
        s_pages.append(jnp.where(chosen[p // ppb], s, NEG_INF))
    m = jnp.max(s_own, axis=1, keepdims=True)
    for s in s_pages:
        m = jnp.maximum(m, jnp.max(s, axis=1, keepdims=True))
    p_own = jnp.exp(s_own - m)
    l = jnp.sum(p_own, axis=1, keepdims=True)
    acc = _dot(p_own.astype(BF16), _pad_rows(_load_heads(vn_ref, t_new), PAGE_SIZE).astype(BF16))
    for p in range(npages):
        pr = jnp.exp(s_pages[p] - m)
        l = l + jnp.sum(pr, axis=1, keepdims=True)
        acc = acc + _dot(pr.astype(BF16), _load_heads(vp[p], PAGE_SIZE).astype(BF16))
    o_ref[...] = _unstack_heads(acc / l, t_new)


def _softmax_pv(s, v):
    m = jnp.max(s, axis=1, keepdims=True)
    p = jnp.exp(s - m)
    l = jnp.sum(p, axis=1, keepdims=True)
    return _dot(p.astype(BF16), v) / l


def _mem_prompt_body(q_ref, mk_ref, mv_ref, o_ref):
    s = _dot_nt(q_ref[...].astype(BF16), mk_ref[...].astype(BF16)) * ATT_SCALE
    o_ref[...] = _softmax_pv(s, mv_ref[...].astype(BF16))


def _mem_prompt(q_all, mk, mv, batch, seq, tq):
    nq = seq // tq
    n_mem = mk.shape[0] // batch
    kv_spec = pl.BlockSpec((n_mem, D_HEAD), lambda b, h, i: (b, h))
    return pl.pallas_call(
        _mem_prompt_body,
        grid=(batch, N_HEADS, nq),
        in_specs=[pl.BlockSpec((tq, D_HEAD), lambda b, h, i: (b * nq + i, h)), kv_spec, kv_spec],
        out_specs=pl.BlockSpec((tq, D_HEAD), lambda b, h, i: (b * nq + i, h)),
        out_shape=jax.ShapeDtypeStruct(q_all.shape, F32),
        compiler_params=_params("arbitrary", "arbitrary", "arbitrary"),
        name="mem_prompt",
    )(q_all, mk, mv)


def _mem_sample_body(q_ref, mk_ref, mv_ref, y_hbm, o_ref, *, nb, t_new, n_mem):
    del y_hbm
    for i in range(nb):
        qb = _stack_heads(q_ref[i * t_new:(i + 1) * t_new, :], t_new).astype(BF16)
        s = _dot_nt(qb, _load_heads(mk_ref, n_mem, (i,)).astype(BF16)) * ATT_SCALE
        pv = _softmax_pv(s, _load_heads(mv_ref, n_mem, (i,)).astype(BF16))
        o_ref[i * t_new:(i + 1) * t_new, :] = _unstack_heads(pv, t_new)


def _mem_sample(layer, q_all, mem_k, mem_v, y_all, rows_p, nseq, t_new, nb):
    n_mem = mem_k.shape[2] // N_HEADS
    blk0 = rows_p // (nb * t_new)
    kv_spec = pl.BlockSpec((None, nb, n_mem * N_HEADS, D_HEAD), lambda i: (layer, i, 0, 0))
    return pl.pallas_call(
        functools.partial(_mem_sample_body, nb=nb, t_new=t_new, n_mem=n_mem),
        grid=(nseq // nb,),
        in_specs=[pl.BlockSpec((nb * t_new, GROUP_W), lambda i: (blk0 + i, 0)), kv_spec, kv_spec,
                  pl.BlockSpec(memory_space=pl.ANY)],
        out_specs=pl.BlockSpec((nb * t_new, GROUP_W), lambda i: (blk0 + i, 0)),
        out_shape=jax.ShapeDtypeStruct(y_all.shape, F32),
        input_output_aliases={3: 0},
        compiler_params=_params("arbitrary"),
        name="mem_sample",
    )(q_all, mem_k, mem_v, y_all)


def _finish_attn_body(y0, y1, y2, y3, x_ref, w_ref, g_ref, b_ref, o_ref, *, alpha):
    acc = alpha * x_ref[...]
    for gi, y_ref in enumerate((y0, y1, y2, y3)):
        acc = acc + _dot(y_ref[...].astype(BF16), w_ref[gi * GROUP_W:(gi + 1) * GROUP_W, :])
    o_ref[...] = _layer_norm(acc, g_ref[...], b_ref[...])


def _finish_attn(layer, parts, x, w_bf16, g, b, alpha, tm):
    rows, d_model = x.shape
    part_spec = pl.BlockSpec((tm, GROUP_W), lambda i: (i, 0))
    vec_spec = pl.BlockSpec((None, 1, d_model), lambda i: (layer, 0, 0))
    return pl.pallas_call(
        functools.partial(_finish_attn_body, alpha=alpha),
        grid=(rows // tm,),
        in_specs=[part_spec] * 4 + [pl.BlockSpec((tm, d_model), lambda i: (i, 0)),
                                    pl.BlockSpec((None,) + w_bf16.shape[1:], lambda i: (layer, 0, 0)),
                                    vec_spec, vec_spec],
        out_specs=pl.BlockSpec((tm, d_model), lambda i: (i, 0)),
        out_shape=jax.ShapeDtypeStruct((rows, d_model), F32),
        compiler_params=_params("arbitrary"),
        name="out_projection_ln",
    )(*parts, x, w_bf16, g, b)


def _ffn_body(x_ref, wu_ref, wd_ref, g_ref, b_ref, o_ref, xb_ref, acc_ref, *, alpha):
    j = pl.program_id(1)

    @pl.when(j == 0)
    def _init():
        xb_ref[...] = x_ref[...].astype(BF16)
        acc_ref[...] = jnp.zeros(acc_ref.shape, F32)

    hid = jnp.square(jnp.maximum(_dot(xb_ref[...], wu_ref[...]), 0.0))
    acc_ref[...] += _dot(hid.astype(BF16), wd_ref[...])

    @pl.when(j == pl.num_programs(1) - 1)
    def _finish():
        o_ref[...] = _layer_norm(alpha * x_ref[...] + acc_ref[...], g_ref[...], b_ref[...])


def _ffn(layer, x, wu_bf16, wd_bf16, g, b, alpha, tm, tf):
    rows, d_model = x.shape
    d_ff = wu_bf16.shape[2]
    vec_spec = pl.BlockSpec((None, 1, d_model), lambda i, j: (layer, 0, 0))
    return pl.pallas_call(
        functools.partial(_ffn_body, alpha=alpha),
        grid=(rows // tm, d_ff // tf),
        in_specs=[pl.BlockSpec((tm, d_model), lambda i, j: (i, 0)),
                  pl.BlockSpec((None, d_model, tf), lambda i, j: (layer, 0, j)),
                  pl.BlockSpec((None, tf, d_model), lambda i, j: (layer, j, 0)), vec_spec, vec_spec],
        out_specs=pl.BlockSpec((tm, d_model), lambda i, j: (i, 0)),
        out_shape=jax.ShapeDtypeStruct((rows, d_model), F32),
        scratch_shapes=[pltpu.VMEM((tm, d_model), BF16), pltpu.VMEM((tm, d_model), F32)],
        compiler_params=_params("arbitrary", "arbitrary"),
        name="ffn_ln",
    )(x, wu_bf16, wd_bf16, g, b)


def _rope_tables(positions):
    half = D_HEAD // 2
    inv = 1.0 / (ROPE_THETA ** (jnp.arange(half, dtype=F32) / half))
    ang = positions.astype(F32)[:, None] * inv[None, :]
    cos, sin = jnp.cos(ang), jnp.sin(ang)
    return jnp.concatenate([cos, cos], axis=1), jnp.concatenate([-sin, sin], axis=1)


def _largest_tile(n, cap):
    t = min(n, cap)
    while n % t:
        t //= 2
    return t


def kernel(x_prompt, x_sample, cache_sb_k, cache_sb_v, cache_mb_k, cache_mb_v, cache_mem_k, cache_mem_v, state_pool, page_table, mem_prompt, w_in, w_mem_k, w_mem_v, pool_w, pool_scale, w_out, ln1_g, ln1_b, w_up, w_down, ln2_g, ln2_b):
    batch, seq, d_model = x_prompt.shape
    nseq, t_new, _ = x_sample.shape
    depth = w_in.shape[0]
    npages = page_table.shape[1]
    past = npages * PAGE_SIZE
    n_mem = mem_prompt.shape[1]
    rows_p, rows_s = batch * seq, nseq * t_new
    rows = rows_p + rows_s
    alpha = float((2 * depth) ** 0.25)
    assert d_model == 4 * GROUP_W and w_in.shape[2] == N_GROUPS_IN * GROUP_W
    assert seq % MOBA_BLOCK == 0 and past % MOBA_BLOCK == 0 and t_new <= PAGE_SIZE and t_new % 8 == 0
    assert state_pool.shape[2] == POOL_BUF and cache_sb_k.shape[2:] == (PAGE_SIZE, N_HEADS, D_HEAD)

    tm = _largest_tile(np.gcd(rows_p, rows_s), 512)
    n_phys = cache_sb_k.shape[1]
    paged = lambda c: c.reshape(depth, n_phys, PAGE_SIZE * N_HEADS, D_HEAD)
    csk, csv, cmk, cmv = paged(cache_sb_k), paged(cache_sb_v), paged(cache_mb_k), paged(cache_mb_v)
    memk = cache_mem_k.reshape(depth, nseq, n_mem * N_HEADS, D_HEAD)
    memv = cache_mem_v.reshape(depth, nseq, n_mem * N_HEADS, D_HEAD)
    mem2d = mem_prompt.reshape(batch * n_mem, d_model)

    pos = jnp.concatenate([jnp.tile(jnp.arange(seq, dtype=jnp.int32), batch),
                           jnp.tile(past + jnp.arange(t_new, dtype=jnp.int32), nseq)])
    cos_tab, sin_tab = _rope_tables(pos)

    w_in_b, w_out_b, w_up_b, w_down_b = (w.astype(BF16) for w in (w_in, w_out, w_up, w_down))
    w_mem_k_b, w_mem_v_b, pool_w_b = w_mem_k.astype(BF16), w_mem_v.astype(BF16), pool_w.astype(BF16)
    vec = lambda v: v.reshape(depth, 1, v.shape[1])
    pool_sc, g1, b1, g2, b2 = vec(pool_scale), vec(ln1_g), vec(ln1_b), vec(ln2_g), vec(ln2_b)
    tm_in = _largest_tile(np.gcd(rows_p, rows_s), MOBA_BLOCK)
    tf = _largest_tile(w_up.shape[2], 1024)

    x = jnp.concatenate([x_prompt.reshape(rows_p, d_model), x_sample.reshape(rows_s, d_model)], axis=0)
    states = None
    mem_states, pool_p, pool_s = [], [], []
    for l in range(depth):
        (u, q_sb, q_mb, q_mem), (k_sb, v_sb, k_mb, v_mb), kmean, states = _in_projection(
            l, depth, x, w_in_b, cos_tab, sin_tab, tm_in, rows_p, states)
        pk_sb, pv_sb, pk_mb, pv_mb, sk_sb, sv_sb, sk_mb, sv_mb = states
        u_s = u[rows_p:].reshape(nseq, t_new, GROUP_W)
        u_ext = jnp.concatenate([jnp.zeros((nseq, POOL_HALO - POOL_BUF, GROUP_W), F32), state_pool[l], u_s], axis=1)
        y_pool = _pool_prompt(l, u, pool_w_b, pool_sc, batch, seq, _largest_tile(seq, 512))
        y_pool = _pool_sample(l, u_ext, pool_w_b, pool_sc, y_pool, past, t_new, rows_p)
        y_sb = _sb_prompt(q_sb, k_sb, v_sb, batch, seq, 256, SB_HEADS_PER_STEP)
        y_mb = _moba_prompt(q_mb, k_mb, v_mb, kmean, batch, seq, MOBA_HEADS_PER_STEP)
        y_sb, y_mb = _paged_sample(l, page_table, q_sb, sk_sb, sv_sb, q_mb, sk_mb, sv_mb,
                                   csk, csv, cmk, cmv, y_sb, y_mb, rows_p, nseq, t_new)
        mk = _matmul(l, mem2d, w_mem_k_b)
        mv = _matmul(l, mem2d, w_mem_v_b)
        y_mem = _mem_prompt(q_mem, mk, mv, batch, seq, _largest_tile(seq, 512))
        y_mem = _mem_sample(l, q_mem, memk, memv, y_mem, rows_p, nseq, t_new, _largest_tile(nseq, 8))
        x = _finish_attn(l, (y_pool, y_sb, y_mb, y_mem), x, w_out_b, g1, b1, alpha, tm)
        x = _ffn(l, x, w_up_b, w_down_b, g2, b2, alpha, tm, tf)
        mem_states.append((mk.reshape(batch, n_mem, N_HEADS, D_HEAD), mv.reshape(batch, n_mem, N_HEADS, D_HEAD)))
        pool_p.append(u[:rows_p].reshape(batch, seq, GROUP_W)[:, seq - POOL_BUF:])
        pool_s.append(u_ext[:, u_ext.shape[1] - POOL_BUF:])

    heads_p = lambda a: a.reshape(depth, batch, seq, N_HEADS, D_HEAD)
    heads_s = lambda a: a.reshape(depth, nseq, t_new, N_HEADS, D_HEAD)
    return (x[:rows_p].reshape(batch, seq, d_model), x[rows_p:].reshape(nseq, t_new, d_model),
            heads_p(pk_sb), heads_p(pv_sb), heads_p(pk_mb), heads_p(pv_mb),
            jnp.stack([m[0] for m in mem_states]), jnp.stack([m[1] for m in mem_states]), jnp.stack(pool_p),
            heads_s(sk_sb), heads_s(sv_sb), heads_s(sk_mb), heads_s(sv_mb), jnp.stack(pool_s))
```

```python
import functools

import numpy as np
import jax
import jax.numpy as jnp
from jax import lax
from jax.experimental import pallas as pl
from jax.experimental.pallas import tpu as pltpu

F32 = jnp.float32
BF16 = jnp.bfloat16

D_HEAD = 128
N_HEADS = 4
GROUP_W = N_HEADS * D_HEAD
N_GROUPS_IN = 8
POOL_WINDOWS = (2, 4, 8, 16)
POOL_BUF = 15
POOL_HALO = 16
MOBA_BLOCK = 256
MOBA_TOPK = 3
PAGE_SIZE = 128
ROPE_THETA = 10000.0
LN_EPS = 1e-5
NEG_INF = -1e30
ATT_SCALE = D_HEAD ** -0.5
SB_HEADS_PER_STEP = 2
MOBA_HEADS_PER_STEP = 4
SB_DEAD_LOG = -120.0
SB_HEAD_PAGES = 2

VMEM_LIMIT_BYTES = 56 * 1024 * 1024


def _params(*sem):
    return pltpu.CompilerParams(dimension_semantics=sem, vmem_limit_bytes=VMEM_LIMIT_BYTES)


def _dot(a, b):
    return jnp.dot(a, b, preferred_element_type=F32)


def _dot_nt(a, b):
    return lax.dot_general(a, b, (((1,), (1,)), ((), ())), preferred_element_type=F32)


def _split_hi_lo(x):
    hi = x.astype(BF16)
    lo = (x - hi.astype(F32)).astype(BF16)
    return hi, lo


def _softplus(z):
    return jnp.maximum(z, 0.0) + jnp.log(1.0 + jnp.exp(-jnp.abs(z)))


def _layer_norm(x, g, b):
    mu = jnp.mean(x, axis=-1, keepdims=True)
    xc = x - mu
    var = jnp.mean(xc * xc, axis=-1, keepdims=True)
    return xc * lax.rsqrt(var + LN_EPS) * g + b


def _head_cols(x, h):
    return x[:, h * D_HEAD:(h + 1) * D_HEAD]


def _load_heads(ref, tokens, lead=()):
    return jnp.concatenate(
        [ref[lead + (pl.ds(h, tokens, stride=N_HEADS), slice(None))] for h in range(N_HEADS)], axis=1)


def _store_heads(ref, val):
    for h in range(N_HEADS):
        ref[pl.ds(h, val.shape[0], stride=N_HEADS), :] = _head_cols(val, h)


def _row_specs(x_parts, tm, nbp):
    width = x_parts[0].shape[1]
    if len(x_parts) == 1:
        return [pl.BlockSpec((tm, width), lambda i, *_: (i, 0))]
    return [pl.BlockSpec((tm, width), lambda i, *_: (jnp.minimum(i, nbp - 1), 0)),
            pl.BlockSpec((tm, width), lambda i, *_: (jnp.maximum(i - nbp, 0), 0))]


def _read_rows(x_refs, i, nbp):
    if len(x_refs) == 1:
        return x_refs[0][...]
    return jnp.where(i < nbp, x_refs[0][...], x_refs[1][...])


def _rope(h, cos, sin):
    return jnp.concatenate(
        [_head_cols(h, c) * cos + pltpu.roll(_head_cols(h, c), D_HEAD // 2, 1) * sin
         for c in range(N_HEADS)], axis=1)


def _inproj_body(*refs, nbp, tm, n_x):
    x_refs, (w_ref, cos_ref, sin_ref) = refs[:n_x], refs[n_x:n_x + 3]
    (u_ref, qsb_ref, qmb_ref, qmem_ref, ksb_ref, vsb_ref, kmb_ref, vmb_ref, kmean_ref,
     pksb_ref, pvsb_ref, pkmb_ref, pvmb_ref, sksb_ref, svsb_ref, skmb_ref, svmb_ref, xb_ref) = refs[-18:]
    i = pl.program_id(0)
    xb_ref[...] = _read_rows(x_refs, i, nbp).astype(BF16)

    def group(g):
        return _dot(xb_ref[...], w_ref[:, g * GROUP_W:(g + 1) * GROUP_W])

    def keep_kv(val, bf_ref, p_ref, s_ref):
        bf_ref[...] = val.astype(BF16)
        pl.when(i < nbp)(lambda: _store_heads(p_ref, val))
        pl.when(i >= nbp)(lambda: _store_heads(s_ref, val))

    u_ref[...] = group(0)
    qsb_ref[...] = group(1)
    keep_kv(group(2), ksb_ref, pksb_ref, sksb_ref)
    keep_kv(group(3), vsb_ref, pvsb_ref, svsb_ref)
    qmb_ref[...] = _rope(group(4), cos_ref[...], sin_ref[...])
    r = _rope(group(5), cos_ref[...], sin_ref[...])
    keep_kv(r, kmb_ref, pkmb_ref, skmb_ref)
    for n in range(tm // MOBA_BLOCK):
        blk = r[n * MOBA_BLOCK:(n + 1) * MOBA_BLOCK, :]
        kmean_ref[n:n + 1, :] = jnp.sum(blk, axis=0, keepdims=True) * (1.0 / MOBA_BLOCK)
    keep_kv(group(6), vmb_ref, pvmb_ref, svmb_ref)
    qmem_ref[...] = group(7)


def _in_projection(layer, depth, x_parts, w_bf16, cos_tab, sin_tab, tm, rows_p, prev_states):
    rows, d_model = sum(x.shape[0] for x in x_parts), x_parts[0].shape[1]
    rows_s = rows - rows_p
    nbp = rows_p // tm
    n_x = len(x_parts)
    assert w_bf16.shape == (depth, d_model, N_GROUPS_IN * GROUP_W)
    assert rows_p % tm == 0 and rows_s % tm == 0 and tm % MOBA_BLOCK == 0
    row_spec = pl.BlockSpec((tm, GROUP_W), lambda i: (i, 0))
    p_spec = pl.BlockSpec((None, tm * N_HEADS, D_HEAD), lambda i: (layer, jnp.minimum(i, nbp - 1), 0))
    s_spec = pl.BlockSpec((None, tm * N_HEADS, D_HEAD), lambda i: (layer, jnp.maximum(i - nbp, 0), 0))
    nkm = tm // MOBA_BLOCK
    f32_sds = jax.ShapeDtypeStruct((rows, GROUP_W), F32)
    bf_sds = jax.ShapeDtypeStruct((rows, GROUP_W), BF16)
    p_sds = jax.ShapeDtypeStruct((depth, rows_p * N_HEADS, D_HEAD), F32)
    s_sds = jax.ShapeDtypeStruct((depth, rows_s * N_HEADS, D_HEAD), F32)
    in_specs = _row_specs(x_parts, tm, nbp) + [
        pl.BlockSpec((None, d_model, N_GROUPS_IN * GROUP_W), lambda i: (layer, 0, 0)),
        pl.BlockSpec((tm, D_HEAD), lambda i: (i, 0)),
        pl.BlockSpec((tm, D_HEAD), lambda i: (i, 0)),
    ]
    args = [*x_parts, w_bf16, cos_tab, sin_tab]
    aliases = {}
    if prev_states is not None:
        in_specs += [pl.BlockSpec(memory_space=pl.ANY)] * 8
        args += list(prev_states)
        aliases = {n_x + 3 + k: 9 + k for k in range(8)}
    outs = pl.pallas_call(
        functools.partial(_inproj_body, nbp=nbp, tm=tm, n_x=n_x),
        grid=(rows // tm,),
        in_specs=in_specs,
        out_specs=[row_spec] * 8 + [pl.BlockSpec((None, nkm, GROUP_W), lambda i: (i, 0, 0))]
        + [p_spec] * 4 + [s_spec] * 4,
        out_shape=[f32_sds] * 4 + [bf_sds] * 4
        + [jax.ShapeDtypeStruct((rows // tm, nkm, GROUP_W), F32)] + [p_sds] * 4 + [s_sds] * 4,
        scratch_shapes=[pltpu.VMEM((tm, d_model), BF16)],
        input_output_aliases=aliases,
        compiler_params=_params("arbitrary"),
        name="in_projection",
    )(*args)
    return outs[0:4], outs[4:8], outs[8].reshape(rows // MOBA_BLOCK, GROUP_W), outs[9:17]


def _mm_body(x_ref, w_ref, o_ref):
    o_ref[...] = _dot(x_ref[...].astype(BF16), w_ref[...])


def _matmul(layer, x, w_bf16):
    m, k = x.shape
    n = w_bf16.shape[2]
    return pl.pallas_call(
        _mm_body,
        grid=(1,),
        in_specs=[pl.BlockSpec((m, k), lambda i: (0, 0)), pl.BlockSpec((None, k, n), lambda i: (layer, 0, 0))],
        out_specs=pl.BlockSpec((m, n), lambda i: (0, 0)),
        out_shape=jax.ShapeDtypeStruct((m, n), F32),
        compiler_params=_params("arbitrary"),
        name="mem_projection",
    )(x, w_bf16)


def _pool_finish(sums, u_groups, counts, pw_ref, sc_ref):
    ys = []
    for g in range(len(POOL_WINDOWS)):
        mixed = sums[g] / counts[g] - u_groups[g]
        ys.append(_dot(mixed.astype(BF16), pw_ref[g]))
    return jnp.concatenate(ys, axis=1) * sc_ref[...]


def _pool_prompt_body(u_ref, pw_ref, sc_ref, o_ref, ext_ref, *, tt):
    i = pl.program_id(1)
    gd = GROUP_W // len(POOL_WINDOWS)

    @pl.when(i == 0)
    def _reset():
        ext_ref[0:POOL_HALO, :] = jnp.zeros((POOL_HALO, GROUP_W), F32)

    ext_ref[POOL_HALO:POOL_HALO + tt, :] = u_ref[...]
    sums = [None] * len(POOL_WINDOWS)
    for s in range(max(POOL_WINDOWS)):
        g0 = min(g for g, w in enumerate(POOL_WINDOWS) if s < w)
        sh = ext_ref[pl.ds(POOL_HALO - s, tt), pl.ds(g0 * gd, GROUP_W - g0 * gd)]
        for g in range(g0, len(POOL_WINDOWS)):
            piece = sh[:, (g - g0) * gd:(g - g0 + 1) * gd]
            sums[g] = piece if sums[g] is None else sums[g] + piece
    pos = i * tt + lax.broadcasted_iota(jnp.int32, (tt, gd), 0)
    counts = [jnp.minimum(w, pos + 1).astype(F32) for w in POOL_WINDOWS]
    u = u_ref[...]
    u_groups = [u[:, g * gd:(g + 1) * gd] for g in range(len(POOL_WINDOWS))]
    o_ref[...] = _pool_finish(sums, u_groups, counts, pw_ref, sc_ref)
    ext_ref[0:POOL_HALO, :] = ext_ref[tt:tt + POOL_HALO, :]


def _pool_prompt(layer, u_all, pw_bf16, scale, batch, seq, tt):
    nt = seq // tt
    return pl.pallas_call(
        functools.partial(_pool_prompt_body, tt=tt),
        grid=(batch, nt),
        in_specs=[
            pl.BlockSpec((tt, GROUP_W), lambda b, i: (b * nt + i, 0)),
            pl.BlockSpec((None,) + pw_bf16.shape[1:], lambda b, i: (layer, 0, 0, 0)),
            pl.BlockSpec((None, 1, GROUP_W), lambda b, i: (layer, 0, 0)),
        ],
        out_specs=pl.BlockSpec((tt, GROUP_W), lambda b, i: (b * nt + i, 0)),
        out_shape=jax.ShapeDtypeStruct(u_all.shape, F32),
        scratch_shapes=[pltpu.VMEM((POOL_HALO + tt, GROUP_W), F32)],
        compiler_params=_params("arbitrary", "arbitrary"),
        name="pool_prompt",
    )(u_all, pw_bf16, scale)


def _pool_sample_body(ue_ref, pw_ref, sc_ref, y_hbm, o_ref, *, past, t_new):
    del y_hbm
    nseq = ue_ref.shape[0]
    gd = GROUP_W // len(POOL_WINDOWS)
    sums = [None] * len(POOL_WINDOWS)
    for s in range(max(POOL_WINDOWS)):
        g0 = min(g for g, w in enumerate(POOL_WINDOWS) if s < w)
        sh = ue_ref[:, pl.ds(POOL_HALO - s, t_new), pl.ds(g0 * gd, GROUP_W - g0 * gd)]
        for g in range(g0, len(POOL_WINDOWS)):
            piece = sh[:, :, (g - g0) * gd:(g - g0 + 1) * gd].reshape(nseq * t_new, gd)
            sums[g] = piece if sums[g] is None else sums[g] + piece
    pos = past + lax.broadcasted_iota(jnp.int32, (nseq, t_new, gd), 1).reshape(nseq * t_new, gd)
    counts = [jnp.minimum(w, pos + 1).astype(F32) for w in POOL_WINDOWS]
    u = ue_ref[:, pl.ds(POOL_HALO, t_new), :]
    u_groups = [u[:, :, g * gd:(g + 1) * gd].reshape(nseq * t_new, gd) for g in range(len(POOL_WINDOWS))]
    o_ref[...] = _pool_finish(sums, u_groups, counts, pw_ref, sc_ref)


def _pool_sample(layer, u_ext, pw_bf16, scale, y_all, past, t_new, rows_p):
    nseq = u_ext.shape[0]
    rows_s = nseq * t_new
    assert rows_p % rows_s == 0
    return pl.pallas_call(
        functools.partial(_pool_sample_body, past=past, t_new=t_new),
        grid=(1,),
        in_specs=[
            pl.BlockSpec(u_ext.shape, lambda i: (0, 0, 0)),
            pl.BlockSpec((None,) + pw_bf16.shape[1:], lambda i: (layer, 0, 0, 0)),
            pl.BlockSpec((None, 1, GROUP_W), lambda i: (layer, 0, 0)),
            pl.BlockSpec(memory_space=pl.ANY),
        ],
        out_specs=pl.BlockSpec((rows_s, GROUP_W), lambda i: (rows_p // rows_s, 0)),
        out_shape=jax.ShapeDtypeStruct(y_all.shape, F32),
        input_output_aliases={3: 0},
        compiler_params=_params("arbitrary"),
        name="pool_sample",
    )(u_ext, pw_bf16, scale, y_all)


def _suffix_sum_matrix(n):
    row = lax.broadcasted_iota(jnp.int32, (n, n), 0)
    col = lax.broadcasted_iota(jnp.int32, (n, n), 1)
    upper = jnp.where(row > col, 1.0, 0.0).astype(BF16)
    return jnp.concatenate([upper, jnp.ones((n, n), BF16)], axis=1)


def _sb_blocks(qs, ks, vs, u2, carries, valid):
    n = ks[0].shape[0]
    zs = [_dot_nt(q, k) * ATT_SCALE for q, k in zip(qs, ks)]
    sps = [_softplus(z) for z in zs]
    log_fails = [-sp if valid is None else jnp.where(valid, -sp, 0.0) for sp in sps]
    csts = [_dot(lf.astype(BF16), u2) for lf in log_fails]
    ws = [jnp.exp(z - sp + cst[:, :n] + c) for z, sp, cst, c in zip(zs, sps, csts, carries)]
    if valid is not None:
        ws = [jnp.where(valid, w, 0.0) for w in ws]
    pvs = [_dot(w.astype(BF16), v) for w, v in zip(ws, vs)]
    return pvs, [c + cst[:, n:] for c, cst in zip(carries, csts)]


def _sb_prompt_body(q_ref, k_ref, v_ref, o_ref, acc_ref, car_ref, live_ref, *, tq, hp):
    qi = pl.program_id(2)
    qs = [_head_cols(q_ref, h).astype(BF16) for h in range(hp)]
    u2 = _suffix_sum_matrix(tq)
    row = lax.broadcasted_iota(jnp.int32, (tq, tq), 0)
    col = lax.broadcasted_iota(jnp.int32, (tq, tq), 1)

    def blocks(kb, carries, valid):
        off = pl.multiple_of(kb * tq, tq)
        ks = [k_ref[pl.ds(off, tq), pl.ds(h * D_HEAD, D_HEAD)] for h in range(hp)]
        vs = [v_ref[pl.ds(off, tq), pl.ds(h * D_HEAD, D_HEAD)] for h in range(hp)]
        return _sb_blocks(qs, ks, vs, u2, carries, valid)

    def keep(pvs, cars, first):
        live = None
        for h in range(hp):
            acc_ref[h] = pvs[h] if first else acc_ref[h] + pvs[h]
            car_ref[h] = cars[h]
            top = jnp.max(cars[h][:, :D_HEAD])
            live = top if live is None else jnp.maximum(live, top)
        live_ref[0] = (live > SB_DEAD_LOG).astype(jnp.int32)

    keep(*blocks(qi, [jnp.zeros((tq, tq), F32)] * hp, col < row), True)

    def past(t):
        keep(*blocks(qi - 1 - t, [car_ref[h] for h in range(hp)], None), False)
        return t + 1

    lax.while_loop(lambda t: (t < qi) & (live_ref[0] > 0), past, 0)
    o_ref[...] = jnp.concatenate([acc_ref[h] for h in range(hp)], axis=1)


def _sb_prompt(q_all, k_bf, v_bf, batch, seq, tq, hp):
    nq = seq // tq
    wide = hp * D_HEAD
    kv_spec = pl.BlockSpec((seq, wide), lambda b, h, i: (b, h))
    return pl.pallas_call(
        functools.partial(_sb_prompt_body, tq=tq, hp=hp),
        grid=(batch, N_HEADS // hp, nq),
        in_specs=[pl.BlockSpec((tq, wide), lambda b, h, i: (b * nq + i, h)), kv_spec, kv_spec],
        out_specs=pl.BlockSpec((tq, wide), lambda b, h, i: (b * nq + i, h)),
        out_shape=jax.ShapeDtypeStruct(q_all.shape, F32),
        scratch_shapes=[pltpu.VMEM((hp, tq, D_HEAD), F32), pltpu.VMEM((hp, tq, tq), F32),
                        pltpu.SMEM((1,), jnp.int32)],
        compiler_params=_params("arbitrary", "arbitrary", "arbitrary"),
        name="sb_prompt",
    )(q_all, k_bf, v_bf)


def _head_masks(t_new):
    rows = N_HEADS * t_new
    r = lax.broadcasted_iota(jnp.int32, (rows, GROUP_W), 0)
    c = lax.broadcasted_iota(jnp.int32, (rows, GROUP_W), 1)
    same = None
    for h in range(N_HEADS):
        m = (r >= h * t_new) & (r < (h + 1) * t_new) & (c >= h * D_HEAD) & (c < (h + 1) * D_HEAD)
        same = m if same is None else same | m
    return same


def _stack_heads(q, t_new):
    qrep = jnp.concatenate([q] * N_HEADS, axis=0)
    return jnp.where(_head_masks(t_new), qrep, 0.0)


def _unstack_heads(acc, t_new):
    masked = jnp.where(_head_masks(t_new), acc, 0.0)
    out = masked[0:t_new]
    for h in range(1, N_HEADS):
        out = out + masked[h * t_new:(h + 1) * t_new]
    return out


def _row_time(t_new, width):
    r = lax.broadcasted_iota(jnp.int32, (N_HEADS * t_new, width), 0)
    t = r
    for h in range(1, N_HEADS):
        t = jnp.where(r >= h * t_new, r - h * t_new, t)
    return t


def _pad_rows(x, rows):
    if rows == x.shape[0]:
        return x
    return jnp.concatenate([x, jnp.zeros((rows - x.shape[0], x.shape[1]), x.dtype)], axis=0)


def _sb_sample_blocks(qb, ks, vs, u2, carry, valid_first):
    rows = qb.shape[0]
    zs = [_dot_nt(qb, k) * ATT_SCALE for k in ks]
    sps = [_softplus(z) for z in zs]
    log_fails = [-sp for sp in sps]
    if valid_first is not None:
        log_fails[0] = jnp.where(valid_first, log_fails[0], 0.0)
    cst = _dot(jnp.concatenate(log_fails, axis=0).astype(BF16), u2)
    acc = None
    for b in range(len(ks)):
        cb = cst[b * rows:(b + 1) * rows]
        w = jnp.exp(zs[b] - sps[b] + cb[:, :PAGE_SIZE] + carry)
        carry = carry + cb[:, PAGE_SIZE:]
        if b == 0 and valid_first is not None:
            w = jnp.where(valid_first, w, 0.0)
        pv = _dot(w.astype(BF16), vs[b]())
        acc = pv if acc is None else acc + pv
    return acc, carry


def _sb_sample_head(q_ref, kn_ref, vn_ref, kp, vp, y_ref, acc_ref, car_ref, live_ref, t_new):
    nhead = len(kp)
    rows = N_HEADS * t_new
    qb = _stack_heads(q_ref[...], t_new).astype(BF16)
    col = lax.broadcasted_iota(jnp.int32, (rows, PAGE_SIZE), 1)
    ks = [_pad_rows(_load_heads(kn_ref, t_new), PAGE_SIZE).astype(BF16)]
    ks += [_load_heads(kp[p], PAGE_SIZE).astype(BF16) for p in range(nhead - 1, -1, -1)]
    vs = [lambda: _pad_rows(_load_heads(vn_ref, t_new), PAGE_SIZE).astype(BF16)]
    vs += [functools.partial(lambda p: _load_heads(vp[p], PAGE_SIZE).astype(BF16), p)
           for p in range(nhead - 1, -1, -1)]
    acc, carry = _sb_sample_blocks(qb, ks, vs, _suffix_sum_matrix(PAGE_SIZE),
                                   jnp.zeros((rows, PAGE_SIZE), F32), col < _row_time(t_new, PAGE_SIZE))
    y_ref[...] = _unstack_heads(acc, t_new)
    acc_ref[...] = acc
    car_ref[...] = carry
    top = jnp.max(carry, axis=0, keepdims=True)
    live_ref[...] = jnp.broadcast_to(top > SB_DEAD_LOG, live_ref.shape).astype(jnp.int32)


def _sb_sample_tail_body(pt_ref, live_ref, q_ref, acc_ref, car_ref, yin_ref, *refs, ntail, t_new):
    kp, vp, o_ref = refs[:ntail], refs[ntail:2 * ntail], refs[-1]
    b = pl.program_id(0)

    @pl.when(live_ref[b] == 0)
    def _done():
        o_ref[...] = yin_ref[...]

    @pl.when(live_ref[b] != 0)
    def _more():
        qb = _stack_heads(q_ref[...], t_new).astype(BF16)
        ks = [_load_heads(kp[p], PAGE_SIZE).astype(BF16) for p in range(ntail - 1, -1, -1)]
        vs = [functools.partial(lambda p: _load_heads(vp[p], PAGE_SIZE).astype(BF16), p)
              for p in range(ntail - 1, -1, -1)]
        acc, _ = _sb_sample_blocks(qb, ks, vs, _suffix_sum_matrix(PAGE_SIZE), car_ref[...], None)
        o_ref[...] = _unstack_heads(acc_ref[...] + acc, t_new)


PAGE_BLOCK = (None, None, PAGE_SIZE * N_HEADS, D_HEAD)


def _paged_specs(layer, npages, pages):
    def spec(p):
        return pl.BlockSpec(PAGE_BLOCK, lambda b, pt: (layer, pt[b * npages + p], 0, 0))
    return [spec(p) for p in pages]


def _paged_sample_body(pt_ref, qsb_ref, ksbn_ref, vsbn_ref, qmb_ref, kmbn_ref, vmbn_ref, *refs,
                       nhead, npages, t_new):
    sbk, sbv = refs[:nhead], refs[nhead:2 * nhead]
    mbk, mbv = refs[2 * nhead:2 * nhead + npages], refs[2 * nhead + npages:2 * nhead + 2 * npages]
    ysb_ref, ymb_ref, acc_ref, car_ref, live_ref = refs[-5:]
    _sb_sample_head(qsb_ref, ksbn_ref, vsbn_ref, sbk, sbv, ysb_ref, acc_ref, car_ref, live_ref, t_new)
    _moba_sample(qmb_ref, kmbn_ref, vmbn_ref, mbk, mbv, ymb_ref, t_new)


def _paged_sample(layer, page_table, q_sb, sb_new_k, sb_new_v, q_mb, mb_new_k, mb_new_v,
                  cache_sb_k, cache_sb_v, cache_mb_k, cache_mb_v, y_sb, y_mb, rows_p, nseq, t_new):
    npages = page_table.shape[1]
    nhead = min(SB_HEAD_PAGES, npages - 1)
    ntail = npages - nhead
    rows = N_HEADS * t_new
    blk0 = rows_p // t_new
    pt_flat = page_table.reshape(-1)
    q_spec = pl.BlockSpec((t_new, GROUP_W), lambda b, pt: (blk0 + b, 0))
    new_spec = pl.BlockSpec((None, t_new * N_HEADS, D_HEAD), lambda b, pt: (layer, b, 0))
    head_pages = list(range(ntail, npages))
    n_in = 7 + 2 * nhead + 2 * npages
    y_sb, y_mb, acc, carry, live = pl.pallas_call(
        functools.partial(_paged_sample_body, nhead=nhead, npages=npages, t_new=t_new),
        grid_spec=pltpu.PrefetchScalarGridSpec(
            num_scalar_prefetch=1,
            grid=(nseq,),
            in_specs=[q_spec, new_spec, new_spec, q_spec, new_spec, new_spec]
            + 2 * _paged_specs(layer, npages, head_pages) + 2 * _paged_specs(layer, npages, range(npages))
            + [pl.BlockSpec(memory_space=pl.ANY)] * 2,
            out_specs=[q_spec, q_spec,
                       pl.BlockSpec((rows, GROUP_W), lambda b, pt: (b, 0)),
                       pl.BlockSpec((rows, PAGE_SIZE), lambda b, pt: (b, 0)),
                       pl.BlockSpec((None, 8, PAGE_SIZE), lambda b, pt: (b, 0, 0))],
        ),
        out_shape=[jax.ShapeDtypeStruct(y_sb.shape, F32), jax.ShapeDtypeStruct(y_mb.shape, F32),
                   jax.ShapeDtypeStruct((nseq * rows, GROUP_W), F32),
                   jax.ShapeDtypeStruct((nseq * rows, PAGE_SIZE), F32),
                   jax.ShapeDtypeStruct((nseq, 8, PAGE_SIZE), jnp.int32)],
        input_output_aliases={n_in: 0, n_in + 1: 1},
        compiler_params=_params("arbitrary"),
        name="paged_sample",
    )(pt_flat, q_sb, sb_new_k, sb_new_v, q_mb, mb_new_k, mb_new_v,
      *([cache_sb_k] * nhead), *([cache_sb_v] * nhead), *([cache_mb_k] * npages), *([cache_mb_v] * npages),
      y_sb, y_mb)
    live = live[:, 0, 0]

    def tail_spec(p):
        return pl.BlockSpec(PAGE_BLOCK, lambda b, pt, lv: (
            layer, jnp.where(lv[b] != 0, pt[b * npages + p], 0), 0, 0))

    tail_specs = [tail_spec(p) for p in range(ntail)]
    row_spec = pl.BlockSpec((t_new, GROUP_W), lambda b, pt, lv: (blk0 + b, 0))

    def tail(y):
        return pl.pallas_call(
            functools.partial(_sb_sample_tail_body, ntail=ntail, t_new=t_new),
            grid_spec=pltpu.PrefetchScalarGridSpec(
                num_scalar_prefetch=2,
                grid=(nseq,),
                in_specs=[row_spec,
                          pl.BlockSpec((rows, GROUP_W), lambda b, pt, lv: (b, 0)),
                          pl.BlockSpec((rows, PAGE_SIZE), lambda b, pt, lv: (b, 0)), row_spec]
                + tail_specs + tail_specs,
                out_specs=row_spec,
            ),
            out_shape=jax.ShapeDtypeStruct(y.shape, F32),
            input_output_aliases={5: 0},
            compiler_params=_params("arbitrary"),
            name="sb_sample_tail",
        )(pt_flat, live, q_sb, acc, carry, y, *([cache_sb_k] * ntail), *([cache_sb_v] * ntail))

    y_sb = lax.cond(jnp.any(live != 0), tail, lambda y: y, y_sb)
    return y_sb, y_mb


def _moba_prompt_body(q_ref, k_ref, v_ref, km_ref, o_ref, *, tq, nb, hp):
    qi = pl.program_id(2)
    nbp = -(-nb // 8) * 8
    blk = lax.broadcasted_iota(jnp.int32, (nbp, tq), 0)
    lane = lax.broadcasted_iota(jnp.int32, (tq, D_HEAD), 1)
    qbs, q_augs = [], []
    for h in range(hp):
        q = _head_cols(q_ref, h)
        q_hi, q_lo = _split_hi_lo(q)
        km_hi, km_lo = _split_hi_lo(_pad_rows(_head_cols(km_ref, h), nbp))
        gate = _dot_nt(km_hi, q_hi) + _dot_nt(km_hi, q_lo) + _dot_nt(km_lo, q_hi)
        gate = jnp.where(blk < qi, gate, NEG_INF)
        rank = jnp.zeros(gate.shape, jnp.int32)
        for m in range(nb):
            gm = gate[m:m + 1, :]
            beats = (gm > gate) | ((gm == gate) & (m < blk))
            rank = rank + beats.astype(jnp.int32)
        bias_t = jnp.where((blk < qi) & (rank < MOBA_TOPK), 0.0, NEG_INF)
        bias = _pad_rows(bias_t, D_HEAD).T
        qbs.append(q.astype(BF16))
        q_augs.append(jnp.concatenate([qbs[h], bias.astype(BF16)], axis=1))

    row = lax.broadcasted_iota(jnp.int32, (tq, tq), 0)
    col = lax.broadcasted_iota(jnp.int32, (tq, tq), 1)

    def kv(n, h):
        off = pl.multiple_of(n * tq, tq)
        cols = pl.ds(h * D_HEAD, D_HEAD)
        return k_ref[pl.ds(off, tq), cols], v_ref[pl.ds(off, tq), cols]

    kvs = [kv(qi, h) for h in range(hp)]
    ss = [jnp.where(col <= row, _dot_nt(qbs[h], kvs[h][0]) * ATT_SCALE, NEG_INF) for h in range(hp)]
    ms = [jnp.max(s, axis=1, keepdims=True) for s in ss]
    ps = [jnp.exp(s - m) for s, m in zip(ss, ms)]
    ls = [jnp.sum(p, axis=1, keepdims=True) for p in ps]
    accs = [_dot(p.astype(BF16), kvs[h][1]) for h, p in enumerate(ps)]

    def past(n, carry):
        ms, ls, accs = carry
        one_hot = jnp.where(lane == n, 1.0, 0.0).astype(BF16)
        kvs = [kv(n, h) for h in range(hp)]
        vs = [v for _, v in kvs]
        ss = [_dot_nt(q_augs[h], jnp.concatenate([kvs[h][0], one_hot], axis=1)) * ATT_SCALE for h in range(hp)]
        m_new = [jnp.maximum(m, jnp.max(s, axis=1, keepdims=True)) for m, s in zip(ms, ss)]
        alphas = [jnp.exp(m - mn) for m, mn in zip(ms, m_new)]
        ps = [jnp.exp(s - mn) for s, mn in zip(ss, m_new)]
        ls = [a * l + jnp.sum(p, axis=1, keepdims=True) for a, l, p in zip(alphas, ls, ps)]
        accs = [a * acc + _dot(p.astype(BF16), v) for a, acc, p, v in zip(alphas, accs, ps, vs)]
        return m_new, ls, accs

    _, ls, accs = lax.fori_loop(0, qi, past, (ms, ls, accs))
    o_ref[...] = jnp.concatenate([acc / l for acc, l in zip(accs, ls)], axis=1)


def _moba_prompt(q_all, k_bf, v_bf, kmean, batch, seq, hp):
    tq = MOBA_BLOCK
    nq = seq // tq
    wide = hp * D_HEAD
    kv_spec = pl.BlockSpec((seq, wide), lambda b, h, i: (b, h))
    return pl.pallas_call(
        functools.partial(_moba_prompt_body, tq=tq, nb=nq, hp=hp),
        grid=(batch, N_HEADS // hp, nq),
        in_specs=[pl.BlockSpec((tq, wide), lambda b, h, i: (b * nq + i, h)), kv_spec, kv_spec,
                  pl.BlockSpec((nq, wide), lambda b, h, i: (b, h))],
        out_specs=pl.BlockSpec((tq, wide), lambda b, h, i: (b * nq + i, h)),
        out_shape=jax.ShapeDtypeStruct(q_all.shape, F32),
        compiler_params=_params("arbitrary", "arbitrary", "arbitrary"),
        name="moba_prompt",
    )(q_all, k_bf, v_bf, kmean)


def _moba_sample(q_ref, kn_ref, vn_ref, kp, vp, o_ref, t_new):
    npages = len(kp)
    rows = N_HEADS * t_new
    ppb = MOBA_BLOCK // PAGE_SIZE
    nblk = npages // ppb
    q_st = _stack_heads(q_ref[...], t_new)
    qb = q_st.astype(BF16)

    k_pages = [_load_heads(kp[p], PAGE_SIZE) for p in range(npages)]
    gates = []
    for n in range(nblk):
        ksum = k_pages[n * ppb].sum(axis=0, keepdims=True)
        for j in range(1, ppb):
            ksum = ksum + k_pages[n * ppb + j].sum(axis=0, keepdims=True)
        gates.append(jnp.sum(q_st * (ksum * (1.0 / MOBA_BLOCK)), axis=1, keepdims=True))
    chosen = []
    for n in range(nblk):
        rank = jnp.zeros((rows, 1), jnp.int32)
        for m in range(nblk):
            if m != n:
                beats = (gates[m] > gates[n]) | ((gates[m] == gates[n]) & (m < n))
                rank = rank + beats.astype(jnp.int32)
        chosen.append(rank < MOBA_TOPK)

    col = lax.broadcasted_iota(jnp.int32, (rows, PAGE_SIZE), 1)
    kn = _pad_rows(_load_heads(kn_ref, t_new), PAGE_SIZE).astype(BF16)
    s_own = jnp.where(col <= _row_time(t_new, PAGE_SIZE), _dot_nt(qb, kn) * ATT_SCALE, NEG_INF)
    s_pages = []
    for p in range(npages):
        s = _dot_nt(qb, k_pages[p].astype(BF16)) * ATT_SCALE
        s_pages.append(jnp.where(chosen[p // ppb], s, NEG_INF))
    m = jnp.max(s_own, axis=1, keepdims=True)
    for s in s_pages:
        m = jnp.maximum(m, jnp.max(s, axis=1, keepdims=True))
    p_own = jnp.exp(s_own - m)
    l = jnp.sum(p_own, axis=1, keepdims=True)
    acc = _dot(p_own.astype(BF16), _pad_rows(_load_heads(vn_ref, t_new), PAGE_SIZE).astype(BF16))
    for p in range(npages):
        pr = jnp.exp(s_pages[p] - m)
        l = l + jnp.sum(pr, axis=1, keepdims=True)
        acc = acc + _dot(pr.astype(BF16), _load_heads(vp[p], PAGE_SIZE).astype(BF16))
    o_ref[...] = _unstack_heads(acc / l, t_new)


def _softmax_pv_streams(ss, vs):
    ms = [jnp.max(s, axis=1, keepdims=True) for s in ss]
    ps = [jnp.exp(s - m) for s, m in zip(ss, ms)]
    ls = [jnp.sum(p, axis=1, keepdims=True) for p in ps]
    return [_dot(p.astype(BF16), v()) / l for p, v, l in zip(ps, vs, ls)]


def _mem_prompt_body(q_ref, mk_ref, mv_ref, o_ref):
    ss = [_dot_nt(_head_cols(q_ref, h).astype(BF16), _head_cols(mk_ref, h).astype(BF16)) * ATT_SCALE
          for h in range(N_HEADS)]
    vs = [functools.partial(lambda h: _head_cols(mv_ref, h).astype(BF16), h) for h in range(N_HEADS)]
    o_ref[...] = jnp.concatenate(_softmax_pv_streams(ss, vs), axis=1)


def _mem_prompt(q_all, mk, mv, batch, seq, tq):
    nq = seq // tq
    n_mem = mk.shape[0] // batch
    kv_spec = pl.BlockSpec((n_mem, GROUP_W), lambda b, i: (b, 0))
    return pl.pallas_call(
        _mem_prompt_body,
        grid=(batch, nq),
        in_specs=[pl.BlockSpec((tq, GROUP_W), lambda b, i: (b * nq + i, 0)), kv_spec, kv_spec],
        out_specs=pl.BlockSpec((tq, GROUP_W), lambda b, i: (b * nq + i, 0)),
        out_shape=jax.ShapeDtypeStruct(q_all.shape, F32),
        compiler_params=_params("arbitrary", "arbitrary"),
        name="mem_prompt",
    )(q_all, mk, mv)


def _mem_sample_body(q_ref, mk_ref, mv_ref, y_hbm, o_ref, *, nb, t_new, n_mem):
    del y_hbm
    qbs = [_stack_heads(q_ref[i * t_new:(i + 1) * t_new, :], t_new).astype(BF16) for i in range(nb)]
    ss = [_dot_nt(qbs[i], _load_heads(mk_ref, n_mem, (i,)).astype(BF16)) * ATT_SCALE for i in range(nb)]
    vs = [functools.partial(lambda i: _load_heads(mv_ref, n_mem, (i,)).astype(BF16), i) for i in range(nb)]
    for i, pv in enumerate(_softmax_pv_streams(ss, vs)):
        o_ref[i * t_new:(i + 1) * t_new, :] = _unstack_heads(pv, t_new)


def _mem_sample(layer, q_all, mem_k, mem_v, y_all, rows_p, nseq, t_new, nb):
    n_mem = mem_k.shape[2] // N_HEADS
    blk0 = rows_p // (nb * t_new)
    kv_spec = pl.BlockSpec((None, nb, n_mem * N_HEADS, D_HEAD), lambda i: (layer, i, 0, 0))
    return pl.pallas_call(
        functools.partial(_mem_sample_body, nb=nb, t_new=t_new, n_mem=n_mem),
        grid=(nseq // nb,),
        in_specs=[pl.BlockSpec((nb * t_new, GROUP_W), lambda i: (blk0 + i, 0)), kv_spec, kv_spec,
                  pl.BlockSpec(memory_space=pl.ANY)],
        out_specs=pl.BlockSpec((nb * t_new, GROUP_W), lambda i: (blk0 + i, 0)),
        out_shape=jax.ShapeDtypeStruct(y_all.shape, F32),
        input_output_aliases={3: 0},
        compiler_params=_params("arbitrary"),
        name="mem_sample",
    )(q_all, mem_k, mem_v, y_all)


def _finish_attn_body(y0, y1, y2, y3, *refs, alpha, nbp):
    x_refs, (w_ref, g_ref, b_ref, o_ref) = refs[:-4], refs[-4:]
    acc = alpha * _read_rows(x_refs, pl.program_id(0), nbp)
    for gi, y_ref in enumerate((y0, y1, y2, y3)):
        acc = acc + _dot(y_ref[...].astype(BF16), w_ref[gi * GROUP_W:(gi + 1) * GROUP_W, :])
    o_ref[...] = _layer_norm(acc, g_ref[...], b_ref[...])


def _finish_attn(layer, parts, x_parts, w_bf16, g, b, alpha, tm, rows_p):
    rows, d_model = parts[0].shape[0], x_parts[0].shape[1]
    nbp = rows_p // tm
    part_spec = pl.BlockSpec((tm, GROUP_W), lambda i: (i, 0))
    vec_spec = pl.BlockSpec((None, 1, d_model), lambda i: (layer, 0, 0))
    return pl.pallas_call(
        functools.partial(_finish_attn_body, alpha=alpha, nbp=nbp),
        grid=(rows // tm,),
        in_specs=[part_spec] * 4 + _row_specs(x_parts, tm, nbp)
        + [pl.BlockSpec((None,) + w_bf16.shape[1:], lambda i: (layer, 0, 0)), vec_spec, vec_spec],
        out_specs=pl.BlockSpec((tm, d_model), lambda i: (i, 0)),
        out_shape=jax.ShapeDtypeStruct((rows, d_model), F32),
        compiler_params=_params("arbitrary"),
        name="out_projection_ln",
    )(*parts, *x_parts, w_bf16, g, b)


def _ffn_body(x_ref, wu_ref, wd_ref, g_ref, b_ref, *refs, alpha, nbp):
    o_refs, (xb_ref, acc_ref) = refs[:-2], refs[-2:]
    i = pl.program_id(0)
    j = pl.program_id(1)

    @pl.when(j == 0)
    def _init():
        xb_ref[...] = x_ref[...].astype(BF16)
        acc_ref[...] = jnp.zeros(acc_ref.shape, F32)

    hid = jnp.square(jnp.maximum(_dot(xb_ref[...], wu_ref[...]), 0.0))
    acc_ref[...] += _dot(hid.astype(BF16), wd_ref[...])

    @pl.when(j == pl.num_programs(1) - 1)
    def _finish():
        y = _layer_norm(alpha * x_ref[...] + acc_ref[...], g_ref[...], b_ref[...])
        if len(o_refs) == 1:
            o_refs[0][...] = y
        else:
            def put(ref):
                ref[...] = y
            pl.when(i < nbp)(functools.partial(put, o_refs[0]))
            pl.when(i >= nbp)(functools.partial(put, o_refs[1]))


def _ffn(layer, x, wu_bf16, wd_bf16, g, b, alpha, tm, tf, rows_p, split_out):
    rows, d_model = x.shape
    d_ff = wu_bf16.shape[2]
    nbp = rows_p // tm
    if split_out:
        out_parts = (jax.ShapeDtypeStruct((rows_p, d_model), F32), jax.ShapeDtypeStruct((rows - rows_p, d_model), F32))
    else:
        out_parts = (jax.ShapeDtypeStruct((rows, d_model), F32),)
    vec_spec = pl.BlockSpec((None, 1, d_model), lambda i, j: (layer, 0, 0))
    return pl.pallas_call(
        functools.partial(_ffn_body, alpha=alpha, nbp=nbp),
        grid=(rows // tm, d_ff // tf),
        in_specs=[pl.BlockSpec((tm, d_model), lambda i, j: (i, 0)),
                  pl.BlockSpec((None, d_model, tf), lambda i, j: (layer, 0, j)),
                  pl.BlockSpec((None, tf, d_model), lambda i, j: (layer, j, 0)), vec_spec, vec_spec],
        out_specs=_row_specs(out_parts, tm, nbp),
        out_shape=list(out_parts),
        scratch_shapes=[pltpu.VMEM((tm, d_model), BF16), pltpu.VMEM((tm, d_model), F32)],
        compiler_params=_params("arbitrary", "arbitrary"),
        name="ffn_ln",
    )(x, wu_bf16, wd_bf16, g, b)


def _rope_tables(positions):
    half = D_HEAD // 2
    inv = 1.0 / (ROPE_THETA ** (jnp.arange(half, dtype=F32) / half))
    ang = positions.astype(F32)[:, None] * inv[None, :]
    cos, sin = jnp.cos(ang), jnp.sin(ang)
    return jnp.concatenate([cos, cos], axis=1), jnp.concatenate([-sin, sin], axis=1)


def _largest_tile(n, cap):
    t = min(n, cap)
    while n % t:
        t //= 2
    return t


def kernel(x_prompt, x_sample, cache_sb_k, cache_sb_v, cache_mb_k, cache_mb_v, cache_mem_k, cache_mem_v, state_pool, page_table, mem_prompt, w_in, w_mem_k, w_mem_v, pool_w, pool_scale, w_out, ln1_g, ln1_b, w_up, w_down, ln2_g, ln2_b):
    batch, seq, d_model = x_prompt.shape
    nseq, t_new, _ = x_sample.shape
    depth = w_in.shape[0]
    npages = page_table.shape[1]
    past = npages * PAGE_SIZE
    n_mem = mem_prompt.shape[1]
    rows_p, rows_s = batch * seq, nseq * t_new
    rows = rows_p + rows_s
    alpha = float((2 * depth) ** 0.25)
    assert d_model == 4 * GROUP_W and w_in.shape[2] == N_GROUPS_IN * GROUP_W
    assert seq % MOBA_BLOCK == 0 and past % MOBA_BLOCK == 0 and t_new <= PAGE_SIZE and t_new % 8 == 0
    assert state_pool.shape[2] == POOL_BUF and cache_sb_k.shape[2:] == (PAGE_SIZE, N_HEADS, D_HEAD)

    tm = _largest_tile(np.gcd(rows_p, rows_s), 512)
    n_phys = cache_sb_k.shape[1]
    paged = lambda c: c.reshape(depth, n_phys, PAGE_SIZE * N_HEADS, D_HEAD)
    csk, csv, cmk, cmv = paged(cache_sb_k), paged(cache_sb_v), paged(cache_mb_k), paged(cache_mb_v)
    memk = cache_mem_k.reshape(depth, nseq, n_mem * N_HEADS, D_HEAD)
    memv = cache_mem_v.reshape(depth, nseq, n_mem * N_HEADS, D_HEAD)
    mem2d = mem_prompt.reshape(batch * n_mem, d_model)

    pos = jnp.concatenate([jnp.tile(jnp.arange(seq, dtype=jnp.int32), batch),
                           jnp.tile(past + jnp.arange(t_new, dtype=jnp.int32), nseq)])
    cos_tab, sin_tab = _rope_tables(pos)

    w_in_b, w_out_b, w_up_b, w_down_b = (w.astype(BF16) for w in (w_in, w_out, w_up, w_down))
    w_mem_k_b, w_mem_v_b, pool_w_b = w_mem_k.astype(BF16), w_mem_v.astype(BF16), pool_w.astype(BF16)
    vec = lambda v: v.reshape(depth, 1, v.shape[1])
    pool_sc, g1, b1, g2, b2 = vec(pool_scale), vec(ln1_g), vec(ln1_b), vec(ln2_g), vec(ln2_b)
    tm_in = _largest_tile(np.gcd(rows_p, rows_s), MOBA_BLOCK)
    tf = _largest_tile(w_up.shape[2], 1024)

    x_parts = (x_prompt.reshape(rows_p, d_model), x_sample.reshape(rows_s, d_model))
    states = None
    mem_states, pool_p, pool_s = [], [], []
    for l in range(depth):
        (u, q_sb, q_mb, q_mem), (k_sb, v_sb, k_mb, v_mb), kmean, states = _in_projection(
            l, depth, x_parts, w_in_b, cos_tab, sin_tab, tm_in, rows_p, states)
        pk_sb, pv_sb, pk_mb, pv_mb, sk_sb, sv_sb, sk_mb, sv_mb = states
        u_s = u[rows_p:].reshape(nseq, t_new, GROUP_W)
        u_ext = jnp.concatenate([jnp.zeros((nseq, POOL_HALO - POOL_BUF, GROUP_W), F32), state_pool[l], u_s], axis=1)
        y_pool = _pool_prompt(l, u, pool_w_b, pool_sc, batch, seq, _largest_tile(seq, 512))
        y_pool = _pool_sample(l, u_ext, pool_w_b, pool_sc, y_pool, past, t_new, rows_p)
        y_sb = _sb_prompt(q_sb, k_sb, v_sb, batch, seq, 256, SB_HEADS_PER_STEP)
        y_mb = _moba_prompt(q_mb, k_mb, v_mb, kmean, batch, seq, MOBA_HEADS_PER_STEP)
        y_sb, y_mb = _paged_sample(l, page_table, q_sb, sk_sb, sv_sb, q_mb, sk_mb, sv_mb,
                                   csk, csv, cmk, cmv, y_sb, y_mb, rows_p, nseq, t_new)
        mk = _matmul(l, mem2d, w_mem_k_b)
        mv = _matmul(l, mem2d, w_mem_v_b)
        y_mem = _mem_prompt(q_mem, mk, mv, batch, seq, _largest_tile(seq, 512))
        y_mem = _mem_sample(l, q_mem, memk, memv, y_mem, rows_p, nseq, t_new, _largest_tile(nseq, 8))
        x = _finish_attn(l, (y_pool, y_sb, y_mb, y_mem), x_parts, w_out_b, g1, b1, alpha, tm, rows_p)
        x_parts = _ffn(l, x, w_up_b, w_down_b, g2, b2, alpha, tm, tf, rows_p, l == depth - 1)
        mem_states.append((mk.reshape(batch, n_mem, N_HEADS, D_HEAD), mv.reshape(batch, n_mem, N_HEADS, D_HEAD)))
        pool_p.append(u[:rows_p].reshape(batch, seq, GROUP_W)[:, seq - POOL_BUF:])
        pool_s.append(u_ext[:, u_ext.shape[1] - POOL_BUF:])

    heads_p = lambda a: a.reshape(depth, batch, seq, N_HEADS, D_HEAD)
    heads_s = lambda a: a.reshape(depth, nseq, t_new, N_HEADS, D_HEAD)
    return (x_parts[0].reshape(batch, seq, d_model), x_parts[1].reshape(nseq, t_new, d_model),
            heads_p(pk_sb), heads_p(pv_sb), heads_p(pk_mb), heads_p(pv_mb),
            jnp.stack([m[0] for m in mem_states]), jnp.stack([m[1] for m in mem_states]), jnp.stack(pool_p),
            heads_s(sk_sb), heads_s(sv_sb), heads_s(sk_mb), heads_s(sv_mb), jnp.stack(pool_s))
```

```python
import functools

import numpy as np
import jax
import jax.numpy as jnp
from jax import lax
from jax.experimental import pallas as pl
from jax.experimental.pallas import tpu as pltpu

F32 = jnp.float32
BF16 = jnp.bfloat16

D_HEAD = 128
N_HEADS = 4
GROUP_W = N_HEADS * D_HEAD
N_GROUPS_IN = 8
POOL_WINDOWS = (2, 4, 8, 16)
POOL_BUF = 15
POOL_HALO = 16
MOBA_BLOCK = 256
MOBA_TOPK = 3
PAGE_SIZE = 128
ROPE_THETA = 10000.0
LN_EPS = 1e-5
NEG_INF = -1e30
ATT_SCALE = D_HEAD ** -0.5
SB_HEADS_PER_STEP = 4
MOBA_HEADS_PER_STEP = 4
SB_DEAD_LOG = -120.0
SB_HEAD_PAGES = 2

VMEM_LIMIT_BYTES = 56 * 1024 * 1024


def _params(*sem):
    return pltpu.CompilerParams(dimension_semantics=sem, vmem_limit_bytes=VMEM_LIMIT_BYTES)


def _dot(a, b):
    return jnp.dot(a, b, preferred_element_type=F32)


def _dot_nt(a, b):
    return lax.dot_general(a, b, (((1,), (1,)), ((), ())), preferred_element_type=F32)


def _split_hi_lo(x):
    hi = x.astype(BF16)
    lo = (x - hi.astype(F32)).astype(BF16)
    return hi, lo


def _softplus(z):
    return jnp.maximum(z, 0.0) + jnp.log(1.0 + jnp.exp(-jnp.abs(z)))


def _layer_norm(x, g, b):
    mu = jnp.mean(x, axis=-1, keepdims=True)
    xc = x - mu
    var = jnp.mean(xc * xc, axis=-1, keepdims=True)
    return xc * lax.rsqrt(var + LN_EPS) * g + b


def _head_cols(x, h):
    return x[:, h * D_HEAD:(h + 1) * D_HEAD]


def _load_heads(ref, tokens, lead=()):
    return jnp.concatenate(
        [ref[lead + (pl.ds(h, tokens, stride=N_HEADS), slice(None))] for h in range(N_HEADS)], axis=1)


def _store_heads(ref, val):
    for h in range(N_HEADS):
        ref[pl.ds(h, val.shape[0], stride=N_HEADS), :] = _head_cols(val, h)


def _row_specs(x_parts, tm, nbp):
    width = x_parts[0].shape[1]
    if len(x_parts) == 1:
        return [pl.BlockSpec((tm, width), lambda i, *_: (i, 0))]
    return [pl.BlockSpec((tm, width), lambda i, *_: (jnp.minimum(i, nbp - 1), 0)),
            pl.BlockSpec((tm, width), lambda i, *_: (jnp.maximum(i - nbp, 0), 0))]


def _read_rows(x_refs, i, nbp):
    if len(x_refs) == 1:
        return x_refs[0][...]
    return jnp.where(i < nbp, x_refs[0][...], x_refs[1][...])


def _rope(h, cos, sin):
    return jnp.concatenate(
        [_head_cols(h, c) * cos + pltpu.roll(_head_cols(h, c), D_HEAD // 2, 1) * sin
         for c in range(N_HEADS)], axis=1)


def _inproj_body(*refs, nbp, tm, n_x):
    x_refs, (w_ref, cos_ref, sin_ref) = refs[:n_x], refs[n_x:n_x + 3]
    (u_ref, qsb_ref, qmb_ref, qmem_ref, ksb_ref, vsb_ref, kmb_ref, vmb_ref, kmean_ref,
     pksb_ref, pvsb_ref, pkmb_ref, pvmb_ref, sksb_ref, svsb_ref, skmb_ref, svmb_ref, xb_ref) = refs[-18:]
    i = pl.program_id(0)
    xb_ref[...] = _read_rows(x_refs, i, nbp).astype(BF16)

    def group(g):
        return _dot(xb_ref[...], w_ref[:, g * GROUP_W:(g + 1) * GROUP_W])

    def keep_kv(val, bf_ref, p_ref, s_ref, bft_ref=None):
        if bf_ref is not None:
            bf_ref[...] = val.astype(BF16)
        if bft_ref is not None:
            bft_ref[...] = val.T.astype(BF16)
        pl.when(i < nbp)(lambda: _store_heads(p_ref, val))
        pl.when(i >= nbp)(lambda: _store_heads(s_ref, val))

    u_ref[...] = group(0)
    qsb_ref[...] = group(1)
    keep_kv(group(2), ksb_ref, pksb_ref, sksb_ref)
    keep_kv(group(3), vsb_ref, pvsb_ref, svsb_ref)
    qmb_ref[...] = _rope(group(4), cos_ref[...], sin_ref[...])
    r = _rope(group(5), cos_ref[...], sin_ref[...])
    keep_kv(r, kmb_ref, pkmb_ref, skmb_ref)
    for n in range(tm // MOBA_BLOCK):
        blk = r[n * MOBA_BLOCK:(n + 1) * MOBA_BLOCK, :]
        kmean_ref[n:n + 1, :] = jnp.sum(blk, axis=0, keepdims=True) * (1.0 / MOBA_BLOCK)
    keep_kv(group(6), None, pvmb_ref, svmb_ref, vmb_ref)
    qmem_ref[...] = group(7)


def _in_projection(layer, depth, x_parts, w_bf16, cos_tab, sin_tab, tm, rows_p, prev_states):
    rows, d_model = sum(x.shape[0] for x in x_parts), x_parts[0].shape[1]
    rows_s = rows - rows_p
    nbp = rows_p // tm
    n_x = len(x_parts)
    assert w_bf16.shape == (depth, d_model, N_GROUPS_IN * GROUP_W)
    assert rows_p % tm == 0 and rows_s % tm == 0 and tm % MOBA_BLOCK == 0
    row_spec = pl.BlockSpec((tm, GROUP_W), lambda i: (i, 0))
    p_spec = pl.BlockSpec((None, tm * N_HEADS, D_HEAD), lambda i: (layer, jnp.minimum(i, nbp - 1), 0))
    s_spec = pl.BlockSpec((None, tm * N_HEADS, D_HEAD), lambda i: (layer, jnp.maximum(i - nbp, 0), 0))
    nkm = tm // MOBA_BLOCK
    f32_sds = jax.ShapeDtypeStruct((rows, GROUP_W), F32)
    bf_sds = jax.ShapeDtypeStruct((rows, GROUP_W), BF16)
    p_sds = jax.ShapeDtypeStruct((depth, rows_p * N_HEADS, D_HEAD), F32)
    s_sds = jax.ShapeDtypeStruct((depth, rows_s * N_HEADS, D_HEAD), F32)
    in_specs = _row_specs(x_parts, tm, nbp) + [
        pl.BlockSpec((None, d_model, N_GROUPS_IN * GROUP_W), lambda i: (layer, 0, 0)),
        pl.BlockSpec((tm, D_HEAD), lambda i: (i, 0)),
        pl.BlockSpec((tm, D_HEAD), lambda i: (i, 0)),
    ]
    args = [*x_parts, w_bf16, cos_tab, sin_tab]
    aliases = {}
    if prev_states is not None:
        in_specs += [pl.BlockSpec(memory_space=pl.ANY)] * 8
        args += list(prev_states)
        aliases = {n_x + 3 + k: 9 + k for k in range(8)}
    outs = pl.pallas_call(
        functools.partial(_inproj_body, nbp=nbp, tm=tm, n_x=n_x),
        grid=(rows // tm,),
        in_specs=in_specs,
        out_specs=[row_spec] * 7 + [pl.BlockSpec((GROUP_W, tm), lambda i: (0, i)),
                                    pl.BlockSpec((None, nkm, GROUP_W), lambda i: (i, 0, 0))]
        + [p_spec] * 4 + [s_spec] * 4,
        out_shape=[f32_sds] * 4 + [bf_sds] * 3 + [jax.ShapeDtypeStruct((GROUP_W, rows), BF16)]
        + [jax.ShapeDtypeStruct((rows // tm, nkm, GROUP_W), F32)] + [p_sds] * 4 + [s_sds] * 4,
        scratch_shapes=[pltpu.VMEM((tm, d_model), BF16)],
        input_output_aliases=aliases,
        compiler_params=_params("arbitrary"),
        name="in_projection",
    )(*args)
    return outs[0:4], outs[4:8], outs[8].reshape(rows // MOBA_BLOCK, GROUP_W), outs[9:17]


def _mm_body(x_ref, w_ref, o_ref):
    o_ref[...] = _dot(x_ref[...].astype(BF16), w_ref[...])


def _matmul(layer, x, w_bf16):
    m, k = x.shape
    n = w_bf16.shape[2]
    return pl.pallas_call(
        _mm_body,
        grid=(1,),
        in_specs=[pl.BlockSpec((m, k), lambda i: (0, 0)), pl.BlockSpec((None, k, n), lambda i: (layer, 0, 0))],
        out_specs=pl.BlockSpec((m, n), lambda i: (0, 0)),
        out_shape=jax.ShapeDtypeStruct((m, n), F32),
        compiler_params=_params("arbitrary"),
        name="mem_projection",
    )(x, w_bf16)


def _pool_finish(sums, u_groups, counts, pw_ref, sc_ref):
    ys = []
    for g in range(len(POOL_WINDOWS)):
        mixed = sums[g] / counts[g] - u_groups[g]
        ys.append(_dot(mixed.astype(BF16), pw_ref[g]))
    return jnp.concatenate(ys, axis=1) * sc_ref[...]


def _pool_prompt_body(u_ref, pw_ref, sc_ref, o_ref, ext_ref, *, tt):
    i = pl.program_id(1)
    gd = GROUP_W // len(POOL_WINDOWS)

    @pl.when(i == 0)
    def _reset():
        ext_ref[0:POOL_HALO, :] = jnp.zeros((POOL_HALO, GROUP_W), F32)

    ext_ref[POOL_HALO:POOL_HALO + tt, :] = u_ref[...]
    sums = [None] * len(POOL_WINDOWS)
    for s in range(max(POOL_WINDOWS)):
        g0 = min(g for g, w in enumerate(POOL_WINDOWS) if s < w)
        sh = ext_ref[pl.ds(POOL_HALO - s, tt), pl.ds(g0 * gd, GROUP_W - g0 * gd)]
        for g in range(g0, len(POOL_WINDOWS)):
            piece = sh[:, (g - g0) * gd:(g - g0 + 1) * gd]
            sums[g] = piece if sums[g] is None else sums[g] + piece
    pos = i * tt + lax.broadcasted_iota(jnp.int32, (tt, gd), 0)
    counts = [jnp.minimum(w, pos + 1).astype(F32) for w in POOL_WINDOWS]
    u = u_ref[...]
    u_groups = [u[:, g * gd:(g + 1) * gd] for g in range(len(POOL_WINDOWS))]
    o_ref[...] = _pool_finish(sums, u_groups, counts, pw_ref, sc_ref)
    ext_ref[0:POOL_HALO, :] = ext_ref[tt:tt + POOL_HALO, :]


def _pool_prompt(layer, u_all, pw_bf16, scale, batch, seq, tt):
    nt = seq // tt
    return pl.pallas_call(
        functools.partial(_pool_prompt_body, tt=tt),
        grid=(batch, nt),
        in_specs=[
            pl.BlockSpec((tt, GROUP_W), lambda b, i: (b * nt + i, 0)),
            pl.BlockSpec((None,) + pw_bf16.shape[1:], lambda b, i: (layer, 0, 0, 0)),
            pl.BlockSpec((None, 1, GROUP_W), lambda b, i: (layer, 0, 0)),
        ],
        out_specs=pl.BlockSpec((tt, GROUP_W), lambda b, i: (b * nt + i, 0)),
        out_shape=jax.ShapeDtypeStruct(u_all.shape, F32),
        scratch_shapes=[pltpu.VMEM((POOL_HALO + tt, GROUP_W), F32)],
        compiler_params=_params("arbitrary", "arbitrary"),
        name="pool_prompt",
    )(u_all, pw_bf16, scale)


def _pool_sample_body(ue_ref, pw_ref, sc_ref, y_hbm, o_ref, *, past, t_new):
    del y_hbm
    nseq = ue_ref.shape[0]
    gd = GROUP_W // len(POOL_WINDOWS)
    sums = [None] * len(POOL_WINDOWS)
    for s in range(max(POOL_WINDOWS)):
        g0 = min(g for g, w in enumerate(POOL_WINDOWS) if s < w)
        sh = ue_ref[:, pl.ds(POOL_HALO - s, t_new), pl.ds(g0 * gd, GROUP_W - g0 * gd)]
        for g in range(g0, len(POOL_WINDOWS)):
            piece = sh[:, :, (g - g0) * gd:(g - g0 + 1) * gd].reshape(nseq * t_new, gd)
            sums[g] = piece if sums[g] is None else sums[g] + piece
    pos = past + lax.broadcasted_iota(jnp.int32, (nseq, t_new, gd), 1).reshape(nseq * t_new, gd)
    counts = [jnp.minimum(w, pos + 1).astype(F32) for w in POOL_WINDOWS]
    u = ue_ref[:, pl.ds(POOL_HALO, t_new), :]
    u_groups = [u[:, :, g * gd:(g + 1) * gd].reshape(nseq * t_new, gd) for g in range(len(POOL_WINDOWS))]
    o_ref[...] = _pool_finish(sums, u_groups, counts, pw_ref, sc_ref)


def _pool_sample(layer, u_ext, pw_bf16, scale, y_all, past, t_new, rows_p):
    nseq = u_ext.shape[0]
    rows_s = nseq * t_new
    assert rows_p % rows_s == 0
    return pl.pallas_call(
        functools.partial(_pool_sample_body, past=past, t_new=t_new),
        grid=(1,),
        in_specs=[
            pl.BlockSpec(u_ext.shape, lambda i: (0, 0, 0)),
            pl.BlockSpec((None,) + pw_bf16.shape[1:], lambda i: (layer, 0, 0, 0)),
            pl.BlockSpec((None, 1, GROUP_W), lambda i: (layer, 0, 0)),
            pl.BlockSpec(memory_space=pl.ANY),
        ],
        out_specs=pl.BlockSpec((rows_s, GROUP_W), lambda i: (rows_p // rows_s, 0)),
        out_shape=jax.ShapeDtypeStruct(y_all.shape, F32),
        input_output_aliases={3: 0},
        compiler_params=_params("arbitrary"),
        name="pool_sample",
    )(u_ext, pw_bf16, scale, y_all)


def _suffix_sum_matrix(n):
    row = lax.broadcasted_iota(jnp.int32, (n, n), 0)
    col = lax.broadcasted_iota(jnp.int32, (n, n), 1)
    upper = jnp.where(row > col, 1.0, 0.0).astype(BF16)
    return jnp.concatenate([upper, jnp.ones((n, n), BF16)], axis=1)


def _sb_blocks(qs, ks, vs, u2, carries, valid):
    n = ks[0].shape[0]
    zs = [_dot_nt(q, k) * ATT_SCALE for q, k in zip(qs, ks)]
    sps = [_softplus(z) for z in zs]
    log_fails = [-sp if valid is None else jnp.where(valid, -sp, 0.0) for sp in sps]
    csts = [_dot(lf.astype(BF16), u2) for lf in log_fails]
    ws = [jnp.exp(z - sp + cst[:, :n] + c) for z, sp, cst, c in zip(zs, sps, csts, carries)]
    if valid is not None:
        ws = [jnp.where(valid, w, 0.0) for w in ws]
    pvs = [_dot(w.astype(BF16), v) for w, v in zip(ws, vs)]
    return pvs, [c + cst[:, n:] for c, cst in zip(carries, csts)]


def _sb_prompt_body(q_ref, k_ref, v_ref, o_ref, acc_ref, car_ref, live_ref, *, tq, hp):
    qi = pl.program_id(2)
    qs = [_head_cols(q_ref, h).astype(BF16) for h in range(hp)]
    u2 = _suffix_sum_matrix(tq)
    row = lax.broadcasted_iota(jnp.int32, (tq, tq), 0)
    col = lax.broadcasted_iota(jnp.int32, (tq, tq), 1)

    def blocks(kb, carries, valid):
        off = pl.multiple_of(kb * tq, tq)
        ks = [k_ref[pl.ds(off, tq), pl.ds(h * D_HEAD, D_HEAD)] for h in range(hp)]
        vs = [v_ref[pl.ds(off, tq), pl.ds(h * D_HEAD, D_HEAD)] for h in range(hp)]
        return _sb_blocks(qs, ks, vs, u2, carries, valid)

    def keep(pvs, cars, first):
        live = None
        for h in range(hp):
            acc_ref[h] = pvs[h] if first else acc_ref[h] + pvs[h]
            car_ref[h] = cars[h]
            top = jnp.max(cars[h][:, :D_HEAD])
            live = top if live is None else jnp.maximum(live, top)
        live_ref[0] = (live > SB_DEAD_LOG).astype(jnp.int32)

    keep(*blocks(qi, [jnp.zeros((tq, tq), F32)] * hp, col < row), True)

    def past(t):
        keep(*blocks(qi - 1 - t, [car_ref[h] for h in range(hp)], None), False)
        return t + 1

    lax.while_loop(lambda t: (t < qi) & (live_ref[0] > 0), past, 0)
    o_ref[...] = jnp.concatenate([acc_ref[h] for h in range(hp)], axis=1)


def _sb_prompt(q_all, k_bf, v_bf, batch, seq, tq, hp):
    nq = seq // tq
    wide = hp * D_HEAD
    kv_spec = pl.BlockSpec((seq, wide), lambda b, h, i: (b, h))
    return pl.pallas_call(
        functools.partial(_sb_prompt_body, tq=tq, hp=hp),
        grid=(batch, N_HEADS // hp, nq),
        in_specs=[pl.BlockSpec((tq, wide), lambda b, h, i: (b * nq + i, h)), kv_spec, kv_spec],
        out_specs=pl.BlockSpec((tq, wide), lambda b, h, i: (b * nq + i, h)),
        out_shape=jax.ShapeDtypeStruct(q_all.shape, F32),
        scratch_shapes=[pltpu.VMEM((hp, tq, D_HEAD), F32), pltpu.VMEM((hp, tq, tq), F32),
                        pltpu.SMEM((1,), jnp.int32)],
        compiler_params=_params("arbitrary", "arbitrary", "arbitrary"),
        name="sb_prompt",
    )(q_all, k_bf, v_bf)


def _head_masks(t_new):
    rows = N_HEADS * t_new
    r = lax.broadcasted_iota(jnp.int32, (rows, GROUP_W), 0)
    c = lax.broadcasted_iota(jnp.int32, (rows, GROUP_W), 1)
    same = None
    for h in range(N_HEADS):
        m = (r >= h * t_new) & (r < (h + 1) * t_new) & (c >= h * D_HEAD) & (c < (h + 1) * D_HEAD)
        same = m if same is None else same | m
    return same


def _stack_heads(q, t_new):
    qrep = jnp.concatenate([q] * N_HEADS, axis=0)
    return jnp.where(_head_masks(t_new), qrep, 0.0)


def _unstack_heads(acc, t_new):
    masked = jnp.where(_head_masks(t_new), acc, 0.0)
    out = masked[0:t_new]
    for h in range(1, N_HEADS):
        out = out + masked[h * t_new:(h + 1) * t_new]
    return out


def _row_time(t_new, width):
    r = lax.broadcasted_iota(jnp.int32, (N_HEADS * t_new, width), 0)
    t = r
    for h in range(1, N_HEADS):
        t = jnp.where(r >= h * t_new, r - h * t_new, t)
    return t


def _pad_rows(x, rows):
    if rows == x.shape[0]:
        return x
    return jnp.concatenate([x, jnp.zeros((rows - x.shape[0], x.shape[1]), x.dtype)], axis=0)


def _sb_sample_blocks(qb, ks, vs, u2, carry, valid_first):
    rows = qb.shape[0]
    zs = [_dot_nt(qb, k) * ATT_SCALE for k in ks]
    sps = [_softplus(z) for z in zs]
    log_fails = [-sp for sp in sps]
    if valid_first is not None:
        log_fails[0] = jnp.where(valid_first, log_fails[0], 0.0)
    cst = _dot(jnp.concatenate(log_fails, axis=0).astype(BF16), u2)
    acc = None
    for b in range(len(ks)):
        cb = cst[b * rows:(b + 1) * rows]
        w = jnp.exp(zs[b] - sps[b] + cb[:, :PAGE_SIZE] + carry)
        carry = carry + cb[:, PAGE_SIZE:]
        if b == 0 and valid_first is not None:
            w = jnp.where(valid_first, w, 0.0)
        pv = _dot(w.astype(BF16), vs[b]())
        acc = pv if acc is None else acc + pv
    return acc, carry


def _sb_sample_head(q_ref, kn_ref, vn_ref, kp, vp, y_ref, acc_ref, car_ref, live_ref, t_new):
    nhead = len(kp)
    rows = N_HEADS * t_new
    qb = _stack_heads(q_ref[...], t_new).astype(BF16)
    col = lax.broadcasted_iota(jnp.int32, (rows, PAGE_SIZE), 1)
    ks = [_pad_rows(_load_heads(kn_ref, t_new), PAGE_SIZE).astype(BF16)]
    ks += [_load_heads(kp[p], PAGE_SIZE).astype(BF16) for p in range(nhead - 1, -1, -1)]
    vs = [lambda: _pad_rows(_load_heads(vn_ref, t_new), PAGE_SIZE).astype(BF16)]
    vs += [functools.partial(lambda p: _load_heads(vp[p], PAGE_SIZE).astype(BF16), p)
           for p in range(nhead - 1, -1, -1)]
    acc, carry = _sb_sample_blocks(qb, ks, vs, _suffix_sum_matrix(PAGE_SIZE),
                                   jnp.zeros((rows, PAGE_SIZE), F32), col < _row_time(t_new, PAGE_SIZE))
    y_ref[...] = _unstack_heads(acc, t_new)
    acc_ref[...] = acc
    car_ref[...] = carry
    top = jnp.max(carry, axis=0, keepdims=True)
    live_ref[...] = jnp.broadcast_to(top > SB_DEAD_LOG, live_ref.shape).astype(jnp.int32)


def _sb_sample_tail_body(pt_ref, live_ref, q_ref, acc_ref, car_ref, yin_ref, *refs, ntail, t_new):
    kp, vp, o_ref = refs[:ntail], refs[ntail:2 * ntail], refs[-1]
    b = pl.program_id(0)

    @pl.when(live_ref[b] == 0)
    def _done():
        o_ref[...] = yin_ref[...]

    @pl.when(live_ref[b] != 0)
    def _more():
        qb = _stack_heads(q_ref[...], t_new).astype(BF16)
        ks = [_load_heads(kp[p], PAGE_SIZE).astype(BF16) for p in range(ntail - 1, -1, -1)]
        vs = [functools.partial(lambda p: _load_heads(vp[p], PAGE_SIZE).astype(BF16), p)
              for p in range(ntail - 1, -1, -1)]
        acc, _ = _sb_sample_blocks(qb, ks, vs, _suffix_sum_matrix(PAGE_SIZE), car_ref[...], None)
        o_ref[...] = _unstack_heads(acc_ref[...] + acc, t_new)


PAGE_BLOCK = (None, None, PAGE_SIZE * N_HEADS, D_HEAD)


def _paged_specs(layer, npages, pages):
    def spec(p):
        return pl.BlockSpec(PAGE_BLOCK, lambda b, pt: (layer, pt[b * npages + p], 0, 0))
    return [spec(p) for p in pages]


def _paged_sample_body(pt_ref, qsb_ref, ksbn_ref, vsbn_ref, qmb_ref, kmbn_ref, vmbn_ref, *refs,
                       nhead, npages, t_new):
    sbk, sbv = refs[:nhead], refs[nhead:2 * nhead]
    mbk, mbv = refs[2 * nhead:2 * nhead + npages], refs[2 * nhead + npages:2 * nhead + 2 * npages]
    ysb_ref, ymb_ref, acc_ref, car_ref, live_ref = refs[-5:]
    _sb_sample_head(qsb_ref, ksbn_ref, vsbn_ref, sbk, sbv, ysb_ref, acc_ref, car_ref, live_ref, t_new)
    _moba_sample(qmb_ref, kmbn_ref, vmbn_ref, mbk, mbv, ymb_ref, t_new)


def _paged_sample(layer, page_table, q_sb, sb_new_k, sb_new_v, q_mb, mb_new_k, mb_new_v,
                  cache_sb_k, cache_sb_v, cache_mb_k, cache_mb_v, y_sb, y_mb, rows_p, nseq, t_new):
    npages = page_table.shape[1]
    nhead = min(SB_HEAD_PAGES, npages - 1)
    ntail = npages - nhead
    rows = N_HEADS * t_new
    blk0 = rows_p // t_new
    pt_flat = page_table.reshape(-1)
    q_spec = pl.BlockSpec((t_new, GROUP_W), lambda b, pt: (blk0 + b, 0))
    new_spec = pl.BlockSpec((None, t_new * N_HEADS, D_HEAD), lambda b, pt: (layer, b, 0))
    head_pages = list(range(ntail, npages))
    n_in = 7 + 2 * nhead + 2 * npages
    y_sb, y_mb, acc, carry, live = pl.pallas_call(
        functools.partial(_paged_sample_body, nhead=nhead, npages=npages, t_new=t_new),
        grid_spec=pltpu.PrefetchScalarGridSpec(
            num_scalar_prefetch=1,
            grid=(nseq,),
            in_specs=[q_spec, new_spec, new_spec, q_spec, new_spec, new_spec]
            + 2 * _paged_specs(layer, npages, head_pages) + 2 * _paged_specs(layer, npages, range(npages))
            + [pl.BlockSpec(memory_space=pl.ANY)] * 2,
            out_specs=[q_spec, q_spec,
                       pl.BlockSpec((rows, GROUP_W), lambda b, pt: (b, 0)),
                       pl.BlockSpec((rows, PAGE_SIZE), lambda b, pt: (b, 0)),
                       pl.BlockSpec((None, 8, PAGE_SIZE), lambda b, pt: (b, 0, 0))],
        ),
        out_shape=[jax.ShapeDtypeStruct(y_sb.shape, F32), jax.ShapeDtypeStruct(y_mb.shape, F32),
                   jax.ShapeDtypeStruct((nseq * rows, GROUP_W), F32),
                   jax.ShapeDtypeStruct((nseq * rows, PAGE_SIZE), F32),
                   jax.ShapeDtypeStruct((nseq, 8, PAGE_SIZE), jnp.int32)],
        input_output_aliases={n_in: 0, n_in + 1: 1},
        compiler_params=_params("arbitrary"),
        name="paged_sample",
    )(pt_flat, q_sb, sb_new_k, sb_new_v, q_mb, mb_new_k, mb_new_v,
      *([cache_sb_k] * nhead), *([cache_sb_v] * nhead), *([cache_mb_k] * npages), *([cache_mb_v] * npages),
      y_sb, y_mb)
    live = live[:, 0, 0]

    def tail_spec(p):
        return pl.BlockSpec(PAGE_BLOCK, lambda b, pt, lv: (
            layer, jnp.where(lv[b] != 0, pt[b * npages + p], 0), 0, 0))

    tail_specs = [tail_spec(p) for p in range(ntail)]
    row_spec = pl.BlockSpec((t_new, GROUP_W), lambda b, pt, lv: (blk0 + b, 0))

    def tail(y):
        return pl.pallas_call(
            functools.partial(_sb_sample_tail_body, ntail=ntail, t_new=t_new),
            grid_spec=pltpu.PrefetchScalarGridSpec(
                num_scalar_prefetch=2,
                grid=(nseq,),
                in_specs=[row_spec,
                          pl.BlockSpec((rows, GROUP_W), lambda b, pt, lv: (b, 0)),
                          pl.BlockSpec((rows, PAGE_SIZE), lambda b, pt, lv: (b, 0)), row_spec]
                + tail_specs + tail_specs,
                out_specs=row_spec,
            ),
            out_shape=jax.ShapeDtypeStruct(y.shape, F32),
            input_output_aliases={5: 0},
            compiler_params=_params("arbitrary"),
            name="sb_sample_tail",
        )(pt_flat, live, q_sb, acc, carry, y, *([cache_sb_k] * ntail), *([cache_sb_v] * ntail))

    y_sb = lax.cond(jnp.any(live != 0), tail, lambda y: y, y_sb)
    return y_sb, y_mb


def _moba_prompt_body(q_ref, k_ref, vt_ref, km_ref, o_ref, *, tq, nb, hp):
    qi = pl.program_id(2)
    nbp = -(-nb // 8) * 8
    blk = lax.broadcasted_iota(jnp.int32, (nbp, tq), 0)
    lane = lax.broadcasted_iota(jnp.int32, (tq, D_HEAD), 1)
    qbs, q_augs = [], []
    for h in range(hp):
        q = _head_cols(q_ref, h)
        q_hi, q_lo = _split_hi_lo(q)
        km_hi, km_lo = _split_hi_lo(_pad_rows(_head_cols(km_ref, h), nbp))
        gate = _dot_nt(km_hi, q_hi) + _dot_nt(km_hi, q_lo) + _dot_nt(km_lo, q_hi)
        gate = jnp.where(blk < qi, gate, NEG_INF)
        rank = jnp.zeros(gate.shape, jnp.int32)
        for m in range(nb):
            gm = gate[m:m + 1, :]
            beats = (gm > gate) | ((gm == gate) & (m < blk))
            rank = rank + beats.astype(jnp.int32)
        bias_t = jnp.where((blk < qi) & (rank < MOBA_TOPK), 0.0, NEG_INF)
        bias = _pad_rows(bias_t, D_HEAD).T
        qbs.append(q.astype(BF16))
        q_augs.append(jnp.concatenate([qbs[h], bias.astype(BF16)], axis=1))

    key = lax.broadcasted_iota(jnp.int32, (tq, tq), 0)
    qry = lax.broadcasted_iota(jnp.int32, (tq, tq), 1)

    def kv(n, h):
        off = pl.multiple_of(n * tq, tq)
        rows = pl.ds(h * D_HEAD, D_HEAD)
        return k_ref[pl.ds(off, tq), rows], vt_ref[rows, pl.ds(off, tq)]

    kvs = [kv(qi, h) for h in range(hp)]
    ss = [jnp.where(key <= qry, _dot_nt(kvs[h][0], qbs[h]) * ATT_SCALE, NEG_INF) for h in range(hp)]
    ms = [jnp.max(s, axis=0, keepdims=True) for s in ss]
    ps = [jnp.exp(s - m) for s, m in zip(ss, ms)]
    ls = [jnp.sum(p, axis=0, keepdims=True) for p in ps]
    accs = [_dot(kvs[h][1], p.astype(BF16)) for h, p in enumerate(ps)]

    def past(n, carry):
        ms, ls, accs = carry
        one_hot = jnp.where(lane == n, 1.0, 0.0).astype(BF16)
        kvs = [kv(n, h) for h in range(hp)]
        ss = [_dot_nt(jnp.concatenate([kvs[h][0], one_hot], axis=1), q_augs[h]) * ATT_SCALE for h in range(hp)]
        m_new = [jnp.maximum(m, jnp.max(s, axis=0, keepdims=True)) for m, s in zip(ms, ss)]
        alphas = [jnp.exp(m - mn) for m, mn in zip(ms, m_new)]
        ps = [jnp.exp(s - mn) for s, mn in zip(ss, m_new)]
        ls = [a * l + jnp.sum(p, axis=0, keepdims=True) for a, l, p in zip(alphas, ls, ps)]
        accs = [a * acc + _dot(kv_[1], p.astype(BF16)) for a, acc, p, kv_ in zip(alphas, accs, ps, kvs)]
        return m_new, ls, accs

    _, ls, accs = lax.fori_loop(0, qi, past, (ms, ls, accs))
    o_ref[...] = jnp.concatenate([(acc / l).T for acc, l in zip(accs, ls)], axis=1)


def _moba_prompt(q_all, k_bf, vt_bf, kmean, batch, seq, hp):
    tq = MOBA_BLOCK
    nq = seq // tq
    wide = hp * D_HEAD
    return pl.pallas_call(
        functools.partial(_moba_prompt_body, tq=tq, nb=nq, hp=hp),
        grid=(batch, N_HEADS // hp, nq),
        in_specs=[pl.BlockSpec((tq, wide), lambda b, h, i: (b * nq + i, h)),
                  pl.BlockSpec((seq, wide), lambda b, h, i: (b, h)),
                  pl.BlockSpec((wide, seq), lambda b, h, i: (h, b)),
                  pl.BlockSpec((nq, wide), lambda b, h, i: (b, h))],
        out_specs=pl.BlockSpec((tq, wide), lambda b, h, i: (b * nq + i, h)),
        out_shape=jax.ShapeDtypeStruct(q_all.shape, F32),
        compiler_params=_params("arbitrary", "arbitrary", "arbitrary"),
        name="moba_prompt",
    )(q_all, k_bf, vt_bf, kmean)


def _moba_sample(q_ref, kn_ref, vn_ref, kp, vp, o_ref, t_new):
    npages = len(kp)
    rows = N_HEADS * t_new
    ppb = MOBA_BLOCK // PAGE_SIZE
    nblk = npages // ppb
    q_st = _stack_heads(q_ref[...], t_new)
    qb = q_st.astype(BF16)

    k_pages = [_load_heads(kp[p], PAGE_SIZE) for p in range(npages)]
    gates = []
    for n in range(nblk):
        ksum = k_pages[n * ppb].sum(axis=0, keepdims=True)
        for j in range(1, ppb):
            ksum = ksum + k_pages[n * ppb + j].sum(axis=0, keepdims=True)
        gates.append(jnp.sum(q_st * (ksum * (1.0 / MOBA_BLOCK)), axis=1, keepdims=True))
    chosen = []
    for n in range(nblk):
        rank = jnp.zeros((rows, 1), jnp.int32)
        for m in range(nblk):
            if m != n:
                beats = (gates[m] > gates[n]) | ((gates[m] == gates[n]) & (m < n))
                rank = rank + beats.astype(jnp.int32)
        chosen.append(rank < MOBA_TOPK)

    col = lax.broadcasted_iota(jnp.int32, (rows, PAGE_SIZE), 1)
    kn = _pad_rows(_load_heads(kn_ref, t_new), PAGE_SIZE).astype(BF16)
    s_own = jnp.where(col <= _row_time(t_new, PAGE_SIZE), _dot_nt(qb, kn) * ATT_SCALE, NEG_INF)
    s_pages = []
    for p in range(npages):
        s = _dot_nt(qb, k_pages[p].astype(BF16)) * ATT_SCALE
        s_pages.append(jnp.where(chosen[p // ppb], s, NEG_INF))
    m = jnp.max(s_own, axis=1, keepdims=True)
    for s in s_pages:
        m = jnp.maximum(m, jnp.max(s, axis=1, keepdims=True))
    p_own = jnp.exp(s_own - m)
    l = jnp.sum(p_own, axis=1, keepdims=True)
    acc = _dot(p_own.astype(BF16), _pad_rows(_load_heads(vn_ref, t_new), PAGE_SIZE).astype(BF16))
    for p in range(npages):
        pr = jnp.exp(s_pages[p] - m)
        l = l + jnp.sum(pr, axis=1, keepdims=True)
        acc = acc + _dot(pr.astype(BF16), _load_heads(vp[p], PAGE_SIZE).astype(BF16))
    o_ref[...] = _unstack_heads(acc / l, t_new)


def _softmax_pv_streams(ss, vs):
    ms = [jnp.max(s, axis=1, keepdims=True) for s in ss]
    ps = [jnp.exp(s - m) for s, m in zip(ss, ms)]
    ls = [jnp.sum(p, axis=1, keepdims=True) for p in ps]
    return [_dot(p.astype(BF16), v()) / l for p, v, l in zip(ps, vs, ls)]


def _mem_prompt_body(q_ref, mk_ref, mv_ref, o_ref):
    ss = [_dot_nt(_head_cols(q_ref, h).astype(BF16), _head_cols(mk_ref, h).astype(BF16)) * ATT_SCALE
          for h in range(N_HEADS)]
    vs = [functools.partial(lambda h: _head_cols(mv_ref, h).astype(BF16), h) for h in range(N_HEADS)]
    o_ref[...] = jnp.concatenate(_softmax_pv_streams(ss, vs), axis=1)


def _mem_prompt(q_all, mk, mv, batch, seq, tq):
    nq = seq // tq
    n_mem = mk.shape[0] // batch
    kv_spec = pl.BlockSpec((n_mem, GROUP_W), lambda b, i: (b, 0))
    return pl.pallas_call(
        _mem_prompt_body,
        grid=(batch, nq),
        in_specs=[pl.BlockSpec((tq, GROUP_W), lambda b, i: (b * nq + i, 0)), kv_spec, kv_spec],
        out_specs=pl.BlockSpec((tq, GROUP_W), lambda b, i: (b * nq + i, 0)),
        out_shape=jax.ShapeDtypeStruct(q_all.shape, F32),
        compiler_params=_params("arbitrary", "arbitrary"),
        name="mem_prompt",
    )(q_all, mk, mv)


def _mem_sample_body(q_ref, mk_ref, mv_ref, y_hbm, o_ref, *, nb, t_new, n_mem):
    del y_hbm
    qbs = [_stack_heads(q_ref[i * t_new:(i + 1) * t_new, :], t_new).astype(BF16) for i in range(nb)]
    ss = [_dot_nt(qbs[i], _load_heads(mk_ref, n_mem, (i,)).astype(BF16)) * ATT_SCALE for i in range(nb)]
    vs = [functools.partial(lambda i: _load_heads(mv_ref, n_mem, (i,)).astype(BF16), i) for i in range(nb)]
    for i, pv in enumerate(_softmax_pv_streams(ss, vs)):
        o_ref[i * t_new:(i + 1) * t_new, :] = _unstack_heads(pv, t_new)


def _mem_sample(layer, q_all, mem_k, mem_v, y_all, rows_p, nseq, t_new, nb):
    n_mem = mem_k.shape[2] // N_HEADS
    blk0 = rows_p // (nb * t_new)
    kv_spec = pl.BlockSpec((None, nb, n_mem * N_HEADS, D_HEAD), lambda i: (layer, i, 0, 0))
    return pl.pallas_call(
        functools.partial(_mem_sample_body, nb=nb, t_new=t_new, n_mem=n_mem),
        grid=(nseq // nb,),
        in_specs=[pl.BlockSpec((nb * t_new, GROUP_W), lambda i: (blk0 + i, 0)), kv_spec, kv_spec,
                  pl.BlockSpec(memory_space=pl.ANY)],
        out_specs=pl.BlockSpec((nb * t_new, GROUP_W), lambda i: (blk0 + i, 0)),
        out_shape=jax.ShapeDtypeStruct(y_all.shape, F32),
        input_output_aliases={3: 0},
        compiler_params=_params("arbitrary"),
        name="mem_sample",
    )(q_all, mem_k, mem_v, y_all)


def _finish_attn_body(y0, y1, y2, y3, *refs, alpha, nbp):
    x_refs, (w_ref, g_ref, b_ref, o_ref) = refs[:-4], refs[-4:]
    acc = alpha * _read_rows(x_refs, pl.program_id(0), nbp)
    for gi, y_ref in enumerate((y0, y1, y2, y3)):
        acc = acc + _dot(y_ref[...].astype(BF16), w_ref[gi * GROUP_W:(gi + 1) * GROUP_W, :])
    o_ref[...] = _layer_norm(acc, g_ref[...], b_ref[...])


def _finish_attn(layer, parts, x_parts, w_bf16, g, b, alpha, tm, rows_p):
    rows, d_model = parts[0].shape[0], x_parts[0].shape[1]
    nbp = rows_p // tm
    part_spec = pl.BlockSpec((tm, GROUP_W), lambda i: (i, 0))
    vec_spec = pl.BlockSpec((None, 1, d_model), lambda i: (layer, 0, 0))
    return pl.pallas_call(
        functools.partial(_finish_attn_body, alpha=alpha, nbp=nbp),
        grid=(rows // tm,),
        in_specs=[part_spec] * 4 + _row_specs(x_parts, tm, nbp)
        + [pl.BlockSpec((None,) + w_bf16.shape[1:], lambda i: (layer, 0, 0)), vec_spec, vec_spec],
        out_specs=pl.BlockSpec((tm, d_model), lambda i: (i, 0)),
        out_shape=jax.ShapeDtypeStruct((rows, d_model), F32),
        compiler_params=_params("arbitrary"),
        name="out_projection_ln",
    )(*parts, *x_parts, w_bf16, g, b)


def _ffn_body(x_ref, wu_ref, wd_ref, g_ref, b_ref, *refs, alpha, nbp):
    o_refs, (xb_ref, acc_ref) = refs[:-2], refs[-2:]
    i = pl.program_id(0)
    j = pl.program_id(1)

    @pl.when(j == 0)
    def _init():
        xb_ref[...] = x_ref[...].astype(BF16)
        acc_ref[...] = jnp.zeros(acc_ref.shape, F32)

    hid = jnp.square(jnp.maximum(_dot(xb_ref[...], wu_ref[...]), 0.0))
    acc_ref[...] += _dot(hid.astype(BF16), wd_ref[...])

    @pl.when(j == pl.num_programs(1) - 1)
    def _finish():
        y = _layer_norm(alpha * x_ref[...] + acc_ref[...], g_ref[...], b_ref[...])
        if len(o_refs) == 1:
            o_refs[0][...] = y
        else:
            def put(ref):
                ref[...] = y
            pl.when(i < nbp)(functools.partial(put, o_refs[0]))
            pl.when(i >= nbp)(functools.partial(put, o_refs[1]))


def _ffn(layer, x, wu_bf16, wd_bf16, g, b, alpha, tm, tf, rows_p, split_out):
    rows, d_model = x.shape
    d_ff = wu_bf16.shape[2]
    nbp = rows_p // tm
    if split_out:
        out_parts = (jax.ShapeDtypeStruct((rows_p, d_model), F32), jax.ShapeDtypeStruct((rows - rows_p, d_model), F32))
    else:
        out_parts = (jax.ShapeDtypeStruct((rows, d_model), F32),)
    vec_spec = pl.BlockSpec((None, 1, d_model), lambda i, j: (layer, 0, 0))
    return pl.pallas_call(
        functools.partial(_ffn_body, alpha=alpha, nbp=nbp),
        grid=(rows // tm, d_ff // tf),
        in_specs=[pl.BlockSpec((tm, d_model), lambda i, j: (i, 0)),
                  pl.BlockSpec((None, d_model, tf), lambda i, j: (layer, 0, j)),
                  pl.BlockSpec((None, tf, d_model), lambda i, j: (layer, j, 0)), vec_spec, vec_spec],
        out_specs=_row_specs(out_parts, tm, nbp),
        out_shape=list(out_parts),
        scratch_shapes=[pltpu.VMEM((tm, d_model), BF16), pltpu.VMEM((tm, d_model), F32)],
        compiler_params=_params("arbitrary", "arbitrary"),
        name="ffn_ln",
    )(x, wu_bf16, wd_bf16, g, b)


def _rope_tables(positions):
    half = D_HEAD // 2
    inv = 1.0 / (ROPE_THETA ** (jnp.arange(half, dtype=F32) / half))
    ang = positions.astype(F32)[:, None] * inv[None, :]
    cos, sin = jnp.cos(ang), jnp.sin(ang)
    return jnp.concatenate([cos, cos], axis=1), jnp.concatenate([-sin, sin], axis=1)


def _largest_tile(n, cap):
    t = min(n, cap)
    while n % t:
        t //= 2
    return t


def kernel(x_prompt, x_sample, cache_sb_k, cache_sb_v, cache_mb_k, cache_mb_v, cache_mem_k, cache_mem_v, state_pool, page_table, mem_prompt, w_in, w_mem_k, w_mem_v, pool_w, pool_scale, w_out, ln1_g, ln1_b, w_up, w_down, ln2_g, ln2_b):
    batch, seq, d_model = x_prompt.shape
    nseq, t_new, _ = x_sample.shape
    depth = w_in.shape[0]
    npages = page_table.shape[1]
    past = npages * PAGE_SIZE
    n_mem = mem_prompt.shape[1]
    rows_p, rows_s = batch * seq, nseq * t_new
    rows = rows_p + rows_s
    alpha = float((2 * depth) ** 0.25)
    assert d_model == 4 * GROUP_W and w_in.shape[2] == N_GROUPS_IN * GROUP_W
    assert seq % MOBA_BLOCK == 0 and past % MOBA_BLOCK == 0 and t_new <= PAGE_SIZE and t_new % 8 == 0
    assert state_pool.shape[2] == POOL_BUF and cache_sb_k.shape[2:] == (PAGE_SIZE, N_HEADS, D_HEAD)

    tm = _largest_tile(np.gcd(rows_p, rows_s), 512)
    n_phys = cache_sb_k.shape[1]
    paged = lambda c: c.reshape(depth, n_phys, PAGE_SIZE * N_HEADS, D_HEAD)
    csk, csv, cmk, cmv = paged(cache_sb_k), paged(cache_sb_v), paged(cache_mb_k), paged(cache_mb_v)
    memk = cache_mem_k.reshape(depth, nseq, n_mem * N_HEADS, D_HEAD)
    memv = cache_mem_v.reshape(depth, nseq, n_mem * N_HEADS, D_HEAD)
    mem2d = mem_prompt.reshape(batch * n_mem, d_model)

    pos = jnp.concatenate([jnp.tile(jnp.arange(seq, dtype=jnp.int32), batch),
                           jnp.tile(past + jnp.arange(t_new, dtype=jnp.int32), nseq)])
    cos_tab, sin_tab = _rope_tables(pos)

    w_in_b, w_out_b, w_up_b, w_down_b = (w.astype(BF16) for w in (w_in, w_out, w_up, w_down))
    w_mem_k_b, w_mem_v_b, pool_w_b = w_mem_k.astype(BF16), w_mem_v.astype(BF16), pool_w.astype(BF16)
    vec = lambda v: v.reshape(depth, 1, v.shape[1])
    pool_sc, g1, b1, g2, b2 = vec(pool_scale), vec(ln1_g), vec(ln1_b), vec(ln2_g), vec(ln2_b)
    tm_in = _largest_tile(np.gcd(rows_p, rows_s), MOBA_BLOCK)
    tf = _largest_tile(w_up.shape[2], 1024)

    x_parts = (x_prompt.reshape(rows_p, d_model), x_sample.reshape(rows_s, d_model))
    states = None
    mem_states, pool_p, pool_s = [], [], []
    for l in range(depth):
        (u, q_sb, q_mb, q_mem), (k_sb, v_sb, k_mb, v_mb), kmean, states = _in_projection(
            l, depth, x_parts, w_in_b, cos_tab, sin_tab, tm_in, rows_p, states)
        pk_sb, pv_sb, pk_mb, pv_mb, sk_sb, sv_sb, sk_mb, sv_mb = states
        u_s = u[rows_p:].reshape(nseq, t_new, GROUP_W)
        u_ext = jnp.concatenate([jnp.zeros((nseq, POOL_HALO - POOL_BUF, GROUP_W), F32), state_pool[l], u_s], axis=1)
        y_pool = _pool_prompt(l, u, pool_w_b, pool_sc, batch, seq, _largest_tile(seq, 512))
        y_pool = _pool_sample(l, u_ext, pool_w_b, pool_sc, y_pool, past, t_new, rows_p)
        y_sb = _sb_prompt(q_sb, k_sb, v_sb, batch, seq, 256, SB_HEADS_PER_STEP)
        y_mb = _moba_prompt(q_mb, k_mb, v_mb, kmean, batch, seq, MOBA_HEADS_PER_STEP)
        y_sb, y_mb = _paged_sample(l, page_table, q_sb, sk_sb, sv_sb, q_mb, sk_mb, sv_mb,
                                   csk, csv, cmk, cmv, y_sb, y_mb, rows_p, nseq, t_new)
        mk = _matmul(l, mem2d, w_mem_k_b)
        mv = _matmul(l, mem2d, w_mem_v_b)
        y_mem = _mem_prompt(q_mem, mk, mv, batch, seq, _largest_tile(seq, 512))
        y_mem = _mem_sample(l, q_mem, memk, memv, y_mem, rows_p, nseq, t_new, _largest_tile(nseq, 8))
        x = _finish_attn(l, (y_pool, y_sb, y_mb, y_mem), x_parts, w_out_b, g1, b1, alpha, tm, rows_p)
        x_parts = _ffn(l, x, w_up_b, w_down_b, g2, b2, alpha, tm, tf, rows_p, l == depth - 1)
        mem_states.append((mk.reshape(batch, n_mem, N_HEADS, D_HEAD), mv.reshape(batch, n_mem, N_HEADS, D_HEAD)))
        pool_p.append(u[:rows_p].reshape(batch, seq, GROUP_W)[:, seq - POOL_BUF:])
        pool_s.append(u_ext[:, u_ext.shape[1] - POOL_BUF:])

    heads_p = lambda a: a.reshape(depth, batch, seq, N_HEADS, D_HEAD)
    heads_s = lambda a: a.reshape(depth, nseq, t_new, N_HEADS, D_HEAD)
    return (x_parts[0].reshape(batch, seq, d_model), x_parts[1].reshape(nseq, t_new, d_model),
            heads_p(pk_sb), heads_p(pv_sb), heads_p(pk_mb), heads_p(pv_mb),
            jnp.stack([m[0] for m in mem_states]), jnp.stack([m[1] for m in mem_states]), jnp.stack(pool_p),
            heads_s(sk_sb), heads_s(sv_sb), heads_s(sk_mb), heads_s(sv_mb), jnp.stack(pool_s))
```

```python
import functools

import numpy as np
import jax
import jax.numpy as jnp
from jax import lax
from jax.experimental import pallas as pl
from jax.experimental.pallas import tpu as pltpu

F32 = jnp.float32
BF16 = jnp.bfloat16

D_HEAD = 128
N_HEADS = 4
GROUP_W = N_HEADS * D_HEAD
N_GROUPS_IN = 8
POOL_WINDOWS = (2, 4, 8, 16)
POOL_BUF = 15
POOL_HALO = 16
MOBA_BLOCK = 256
MOBA_TOPK = 3
PAGE_SIZE = 128
ROPE_THETA = 10000.0
LN_EPS = 1e-5
NEG_INF = -1e30
ATT_SCALE = D_HEAD ** -0.5
SB_HEADS_PER_STEP = 4
MOBA_HEADS_PER_STEP = 4
SB_DEAD_LOG = -120.0
SB_HEAD_PAGES = 2

VMEM_LIMIT_BYTES = 56 * 1024 * 1024


def _params(*sem):
    return pltpu.CompilerParams(dimension_semantics=sem, vmem_limit_bytes=VMEM_LIMIT_BYTES)


def _dot(a, b):
    return jnp.dot(a, b, preferred_element_type=F32)


def _dot_nt(a, b):
    return lax.dot_general(a, b, (((1,), (1,)), ((), ())), preferred_element_type=F32)


def _split_hi_lo(x):
    hi = x.astype(BF16)
    lo = (x - hi.astype(F32)).astype(BF16)
    return hi, lo


def _softplus(z):
    return jnp.maximum(z, 0.0) + jnp.log(1.0 + jnp.exp(-jnp.abs(z)))


def _layer_norm(x, g, b):
    mu = jnp.mean(x, axis=-1, keepdims=True)
    xc = x - mu
    var = jnp.mean(xc * xc, axis=-1, keepdims=True)
    return xc * lax.rsqrt(var + LN_EPS) * g + b


def _head_cols(x, h):
    return x[:, h * D_HEAD:(h + 1) * D_HEAD]


def _load_heads(ref, tokens, lead=()):
    return jnp.concatenate(
        [ref[lead + (pl.ds(h, tokens, stride=N_HEADS), slice(None))] for h in range(N_HEADS)], axis=1)


def _store_heads(ref, val):
    for h in range(N_HEADS):
        ref[pl.ds(h, val.shape[0], stride=N_HEADS), :] = _head_cols(val, h)


def _row_specs(x_parts, tm, nbp):
    width = x_parts[0].shape[1]
    if len(x_parts) == 1:
        return [pl.BlockSpec((tm, width), lambda i, *_: (i, 0))]
    return [pl.BlockSpec((tm, width), lambda i, *_: (jnp.minimum(i, nbp - 1), 0)),
            pl.BlockSpec((tm, width), lambda i, *_: (jnp.maximum(i - nbp, 0), 0))]


def _read_rows(x_refs, i, nbp):
    if len(x_refs) == 1:
        return x_refs[0][...]
    return jnp.where(i < nbp, x_refs[0][...], x_refs[1][...])


def _rope(h, cos, sin):
    return jnp.concatenate(
        [_head_cols(h, c) * cos + pltpu.roll(_head_cols(h, c), D_HEAD // 2, 1) * sin
         for c in range(N_HEADS)], axis=1)


def _inproj_body(*refs, nbp, tm, n_x):
    x_refs, (w_ref, cos_ref, sin_ref) = refs[:n_x], refs[n_x:n_x + 3]
    (u_ref, qsb_ref, qmb_ref, qmem_ref, ksb_ref, vsb_ref, kmb_ref, vmb_ref, kmean_ref,
     pksb_ref, pvsb_ref, pkmb_ref, pvmb_ref, sksb_ref, svsb_ref, skmb_ref, svmb_ref, xb_ref) = refs[-18:]
    i = pl.program_id(0)
    xb_ref[...] = _read_rows(x_refs, i, nbp).astype(BF16)

    def group(g):
        return _dot(xb_ref[...], w_ref[:, g * GROUP_W:(g + 1) * GROUP_W])

    def keep_kv(val, bf_ref, p_ref, s_ref, bft_ref=None):
        if bf_ref is not None:
            bf_ref[...] = val.astype(BF16)
        if bft_ref is not None:
            bft_ref[...] = val.T.astype(BF16)
        pl.when(i < nbp)(lambda: _store_heads(p_ref, val))
        pl.when(i >= nbp)(lambda: _store_heads(s_ref, val))

    u_ref[...] = group(0)
    qsb_ref[...] = group(1)
    keep_kv(group(2), ksb_ref, pksb_ref, sksb_ref)
    keep_kv(group(3), vsb_ref, pvsb_ref, svsb_ref)
    qmb_ref[...] = _rope(group(4), cos_ref[...], sin_ref[...])
    r = _rope(group(5), cos_ref[...], sin_ref[...])
    keep_kv(r, kmb_ref, pkmb_ref, skmb_ref)
    for n in range(tm // MOBA_BLOCK):
        blk = r[n * MOBA_BLOCK:(n + 1) * MOBA_BLOCK, :]
        kmean_ref[n:n + 1, :] = jnp.sum(blk, axis=0, keepdims=True) * (1.0 / MOBA_BLOCK)
    keep_kv(group(6), None, pvmb_ref, svmb_ref, vmb_ref)
    qmem_ref[...] = group(7)


def _in_projection(layer, depth, x_parts, w_bf16, cos_tab, sin_tab, tm, rows_p, prev_states):
    rows, d_model = sum(x.shape[0] for x in x_parts), x_parts[0].shape[1]
    rows_s = rows - rows_p
    nbp = rows_p // tm
    n_x = len(x_parts)
    assert w_bf16.shape == (depth, d_model, N_GROUPS_IN * GROUP_W)
    assert rows_p % tm == 0 and rows_s % tm == 0 and tm % MOBA_BLOCK == 0
    row_spec = pl.BlockSpec((tm, GROUP_W), lambda i: (i, 0))
    p_spec = pl.BlockSpec((None, tm * N_HEADS, D_HEAD), lambda i: (layer, jnp.minimum(i, nbp - 1), 0))
    s_spec = pl.BlockSpec((None, tm * N_HEADS, D_HEAD), lambda i: (layer, jnp.maximum(i - nbp, 0), 0))
    nkm = tm // MOBA_BLOCK
    f32_sds = jax.ShapeDtypeStruct((rows, GROUP_W), F32)
    bf_sds = jax.ShapeDtypeStruct((rows, GROUP_W), BF16)
    p_sds = jax.ShapeDtypeStruct((depth, rows_p * N_HEADS, D_HEAD), F32)
    s_sds = jax.ShapeDtypeStruct((depth, rows_s * N_HEADS, D_HEAD), F32)
    spb = cos_tab.shape[0] // tm - 1
    rope_spec = pl.BlockSpec((tm, D_HEAD), lambda i: (jnp.where(i < nbp, i % spb, spb), 0))
    in_specs = _row_specs(x_parts, tm, nbp) + [
        pl.BlockSpec((None, d_model, N_GROUPS_IN * GROUP_W), lambda i: (layer, 0, 0)),
        rope_spec, rope_spec,
    ]
    args = [*x_parts, w_bf16, cos_tab, sin_tab]
    aliases = {}
    if prev_states is not None:
        in_specs += [pl.BlockSpec(memory_space=pl.ANY)] * 8
        args += list(prev_states)
        aliases = {n_x + 3 + k: 9 + k for k in range(8)}
    outs = pl.pallas_call(
        functools.partial(_inproj_body, nbp=nbp, tm=tm, n_x=n_x),
        grid=(rows // tm,),
        in_specs=in_specs,
        out_specs=[row_spec] * 7 + [pl.BlockSpec((GROUP_W, tm), lambda i: (0, i)),
                                    pl.BlockSpec((None, nkm, GROUP_W), lambda i: (i, 0, 0))]
        + [p_spec] * 4 + [s_spec] * 4,
        out_shape=[f32_sds] * 4 + [bf_sds] * 3 + [jax.ShapeDtypeStruct((GROUP_W, rows), BF16)]
        + [jax.ShapeDtypeStruct((rows // tm, nkm, GROUP_W), F32)] + [p_sds] * 4 + [s_sds] * 4,
        scratch_shapes=[pltpu.VMEM((tm, d_model), BF16)],
        input_output_aliases=aliases,
        compiler_params=_params("arbitrary"),
        name="in_projection",
    )(*args)
    return outs[0:4], outs[4:8], outs[8].reshape(rows // MOBA_BLOCK, GROUP_W), outs[9:17]


def _mm_body(x_ref, w_ref, o_ref):
    o_ref[...] = _dot(x_ref[...].astype(BF16), w_ref[...])


def _matmul(layer, x, w_bf16):
    m, k = x.shape
    n = w_bf16.shape[2]
    return pl.pallas_call(
        _mm_body,
        grid=(1,),
        in_specs=[pl.BlockSpec((m, k), lambda i: (0, 0)), pl.BlockSpec((None, k, n), lambda i: (layer, 0, 0))],
        out_specs=pl.BlockSpec((m, n), lambda i: (0, 0)),
        out_shape=jax.ShapeDtypeStruct((m, n), F32),
        compiler_params=_params("arbitrary"),
        name="mem_projection",
    )(x, w_bf16)


def _pool_finish(sums, u_groups, counts, pw_ref, sc_ref):
    ys = []
    for g in range(len(POOL_WINDOWS)):
        mixed = sums[g] / counts[g] - u_groups[g]
        ys.append(_dot(mixed.astype(BF16), pw_ref[g]))
    return jnp.concatenate(ys, axis=1) * sc_ref[...]


def _pool_prompt_body(u_ref, pw_ref, sc_ref, o_ref, ext_ref, *, tt):
    i = pl.program_id(1)
    gd = GROUP_W // len(POOL_WINDOWS)

    @pl.when(i == 0)
    def _reset():
        ext_ref[0:POOL_HALO, :] = jnp.zeros((POOL_HALO, GROUP_W), F32)

    ext_ref[POOL_HALO:POOL_HALO + tt, :] = u_ref[...]
    sums = [None] * len(POOL_WINDOWS)
    for s in range(max(POOL_WINDOWS)):
        g0 = min(g for g, w in enumerate(POOL_WINDOWS) if s < w)
        sh = ext_ref[pl.ds(POOL_HALO - s, tt), pl.ds(g0 * gd, GROUP_W - g0 * gd)]
        for g in range(g0, len(POOL_WINDOWS)):
            piece = sh[:, (g - g0) * gd:(g - g0 + 1) * gd]
            sums[g] = piece if sums[g] is None else sums[g] + piece
    pos = i * tt + lax.broadcasted_iota(jnp.int32, (tt, gd), 0)
    counts = [jnp.minimum(w, pos + 1).astype(F32) for w in POOL_WINDOWS]
    u = u_ref[...]
    u_groups = [u[:, g * gd:(g + 1) * gd] for g in range(len(POOL_WINDOWS))]
    o_ref[...] = _pool_finish(sums, u_groups, counts, pw_ref, sc_ref)
    ext_ref[0:POOL_HALO, :] = ext_ref[tt:tt + POOL_HALO, :]


def _pool_prompt(layer, u_all, pw_bf16, scale, batch, seq, tt):
    nt = seq // tt
    return pl.pallas_call(
        functools.partial(_pool_prompt_body, tt=tt),
        grid=(batch, nt),
        in_specs=[
            pl.BlockSpec((tt, GROUP_W), lambda b, i: (b * nt + i, 0)),
            pl.BlockSpec((None,) + pw_bf16.shape[1:], lambda b, i: (layer, 0, 0, 0)),
            pl.BlockSpec((None, 1, GROUP_W), lambda b, i: (layer, 0, 0)),
        ],
        out_specs=pl.BlockSpec((tt, GROUP_W), lambda b, i: (b * nt + i, 0)),
        out_shape=jax.ShapeDtypeStruct(u_all.shape, F32),
        scratch_shapes=[pltpu.VMEM((POOL_HALO + tt, GROUP_W), F32)],
        compiler_params=_params("arbitrary", "arbitrary"),
        name="pool_prompt",
    )(u_all, pw_bf16, scale)


def _pool_sample_body(ue_ref, pw_ref, sc_ref, y_hbm, o_ref, *, past, t_new):
    del y_hbm
    nseq = ue_ref.shape[0]
    gd = GROUP_W // len(POOL_WINDOWS)
    sums = [None] * len(POOL_WINDOWS)
    for s in range(max(POOL_WINDOWS)):
        g0 = min(g for g, w in enumerate(POOL_WINDOWS) if s < w)
        sh = ue_ref[:, pl.ds(POOL_HALO - s, t_new), pl.ds(g0 * gd, GROUP_W - g0 * gd)]
        for g in range(g0, len(POOL_WINDOWS)):
            piece = sh[:, :, (g - g0) * gd:(g - g0 + 1) * gd].reshape(nseq * t_new, gd)
            sums[g] = piece if sums[g] is None else sums[g] + piece
    pos = past + lax.broadcasted_iota(jnp.int32, (nseq, t_new, gd), 1).reshape(nseq * t_new, gd)
    counts = [jnp.minimum(w, pos + 1).astype(F32) for w in POOL_WINDOWS]
    u = ue_ref[:, pl.ds(POOL_HALO, t_new), :]
    u_groups = [u[:, :, g * gd:(g + 1) * gd].reshape(nseq * t_new, gd) for g in range(len(POOL_WINDOWS))]
    o_ref[...] = _pool_finish(sums, u_groups, counts, pw_ref, sc_ref)


def _pool_sample(layer, u_ext, pw_bf16, scale, y_all, past, t_new, rows_p):
    nseq = u_ext.shape[0]
    rows_s = nseq * t_new
    assert rows_p % rows_s == 0
    return pl.pallas_call(
        functools.partial(_pool_sample_body, past=past, t_new=t_new),
        grid=(1,),
        in_specs=[
            pl.BlockSpec(u_ext.shape, lambda i: (0, 0, 0)),
            pl.BlockSpec((None,) + pw_bf16.shape[1:], lambda i: (layer, 0, 0, 0)),
            pl.BlockSpec((None, 1, GROUP_W), lambda i: (layer, 0, 0)),
            pl.BlockSpec(memory_space=pl.ANY),
        ],
        out_specs=pl.BlockSpec((rows_s, GROUP_W), lambda i: (rows_p // rows_s, 0)),
        out_shape=jax.ShapeDtypeStruct(y_all.shape, F32),
        input_output_aliases={3: 0},
        compiler_params=_params("arbitrary"),
        name="pool_sample",
    )(u_ext, pw_bf16, scale, y_all)


def _suffix_sum_matrix(n):
    row = lax.broadcasted_iota(jnp.int32, (n, n), 0)
    col = lax.broadcasted_iota(jnp.int32, (n, n), 1)
    upper = jnp.where(row > col, 1.0, 0.0).astype(BF16)
    return jnp.concatenate([upper, jnp.ones((n, n), BF16)], axis=1)


def _sb_blocks(qs, ks, vs, u2, carries, valid):
    n = ks[0].shape[0]
    zs = [_dot_nt(q, k) * ATT_SCALE for q, k in zip(qs, ks)]
    sps = [_softplus(z) for z in zs]
    log_fails = [-sp if valid is None else jnp.where(valid, -sp, 0.0) for sp in sps]
    csts = [_dot(lf.astype(BF16), u2) for lf in log_fails]
    ws = [jnp.exp(z - sp + cst[:, :n] + c) for z, sp, cst, c in zip(zs, sps, csts, carries)]
    if valid is not None:
        ws = [jnp.where(valid, w, 0.0) for w in ws]
    pvs = [_dot(w.astype(BF16), v) for w, v in zip(ws, vs)]
    return pvs, [c + cst[:, n:] for c, cst in zip(carries, csts)]


def _sb_prompt_body(q_ref, k_ref, v_ref, o_ref, acc_ref, car_ref, live_ref, *, tq, hp):
    qi = pl.program_id(2)
    qs = [_head_cols(q_ref, h).astype(BF16) for h in range(hp)]
    u2 = _suffix_sum_matrix(tq)
    row = lax.broadcasted_iota(jnp.int32, (tq, tq), 0)
    col = lax.broadcasted_iota(jnp.int32, (tq, tq), 1)

    def blocks(kb, carries, valid):
        off = pl.multiple_of(kb * tq, tq)
        ks = [k_ref[pl.ds(off, tq), pl.ds(h * D_HEAD, D_HEAD)] for h in range(hp)]
        vs = [v_ref[pl.ds(off, tq), pl.ds(h * D_HEAD, D_HEAD)] for h in range(hp)]
        return _sb_blocks(qs, ks, vs, u2, carries, valid)

    def keep(pvs, cars, first):
        live = None
        for h in range(hp):
            acc_ref[h] = pvs[h] if first else acc_ref[h] + pvs[h]
            car_ref[h] = cars[h]
            top = jnp.max(cars[h][:, :D_HEAD])
            live = top if live is None else jnp.maximum(live, top)
        live_ref[0] = (live > SB_DEAD_LOG).astype(jnp.int32)

    keep(*blocks(qi, [jnp.zeros((tq, tq), F32)] * hp, col < row), True)

    def past(t):
        keep(*blocks(qi - 1 - t, [car_ref[h] for h in range(hp)], None), False)
        return t + 1

    lax.while_loop(lambda t: (t < qi) & (live_ref[0] > 0), past, 0)
    o_ref[...] = jnp.concatenate([acc_ref[h] for h in range(hp)], axis=1)


def _sb_prompt(q_all, k_bf, v_bf, batch, seq, tq, hp):
    nq = seq // tq
    wide = hp * D_HEAD
    kv_spec = pl.BlockSpec((seq, wide), lambda b, h, i: (b, h))
    return pl.pallas_call(
        functools.partial(_sb_prompt_body, tq=tq, hp=hp),
        grid=(batch, N_HEADS // hp, nq),
        in_specs=[pl.BlockSpec((tq, wide), lambda b, h, i: (b * nq + i, h)), kv_spec, kv_spec],
        out_specs=pl.BlockSpec((tq, wide), lambda b, h, i: (b * nq + i, h)),
        out_shape=jax.ShapeDtypeStruct(q_all.shape, F32),
        scratch_shapes=[pltpu.VMEM((hp, tq, D_HEAD), F32), pltpu.VMEM((hp, tq, tq), F32),
                        pltpu.SMEM((1,), jnp.int32)],
        compiler_params=_params("arbitrary", "arbitrary", "arbitrary"),
        name="sb_prompt",
    )(q_all, k_bf, v_bf)


def _head_masks(t_new):
    rows = N_HEADS * t_new
    r = lax.broadcasted_iota(jnp.int32, (rows, GROUP_W), 0)
    c = lax.broadcasted_iota(jnp.int32, (rows, GROUP_W), 1)
    same = None
    for h in range(N_HEADS):
        m = (r >= h * t_new) & (r < (h + 1) * t_new) & (c >= h * D_HEAD) & (c < (h + 1) * D_HEAD)
        same = m if same is None else same | m
    return same


def _stack_heads(q, t_new):
    qrep = jnp.concatenate([q] * N_HEADS, axis=0)
    return jnp.where(_head_masks(t_new), qrep, 0.0)


def _unstack_heads(acc, t_new):
    masked = jnp.where(_head_masks(t_new), acc, 0.0)
    out = masked[0:t_new]
    for h in range(1, N_HEADS):
        out = out + masked[h * t_new:(h + 1) * t_new]
    return out


def _row_time(t_new, width):
    r = lax.broadcasted_iota(jnp.int32, (N_HEADS * t_new, width), 0)
    t = r
    for h in range(1, N_HEADS):
        t = jnp.where(r >= h * t_new, r - h * t_new, t)
    return t


def _pad_rows(x, rows):
    if rows == x.shape[0]:
        return x
    return jnp.concatenate([x, jnp.zeros((rows - x.shape[0], x.shape[1]), x.dtype)], axis=0)


def _sb_sample_blocks(qb, ks, vs, u2, carry, valid_first):
    rows = qb.shape[0]
    zs = [_dot_nt(qb, k) * ATT_SCALE for k in ks]
    sps = [_softplus(z) for z in zs]
    log_fails = [-sp for sp in sps]
    if valid_first is not None:
        log_fails[0] = jnp.where(valid_first, log_fails[0], 0.0)
    cst = _dot(jnp.concatenate(log_fails, axis=0).astype(BF16), u2)
    acc = None
    for b in range(len(ks)):
        cb = cst[b * rows:(b + 1) * rows]
        w = jnp.exp(zs[b] - sps[b] + cb[:, :PAGE_SIZE] + carry)
        carry = carry + cb[:, PAGE_SIZE:]
        if b == 0 and valid_first is not None:
            w = jnp.where(valid_first, w, 0.0)
        pv = _dot(w.astype(BF16), vs[b]())
        acc = pv if acc is None else acc + pv
    return acc, carry


def _sb_sample_head(q_ref, kn_ref, vn_ref, kp, vp, y_ref, acc_ref, car_ref, live_ref, t_new):
    nhead = len(kp)
    rows = N_HEADS * t_new
    qb = _stack_heads(q_ref[...], t_new).astype(BF16)
    col = lax.broadcasted_iota(jnp.int32, (rows, PAGE_SIZE), 1)
    ks = [_pad_rows(_load_heads(kn_ref, t_new), PAGE_SIZE).astype(BF16)]
    ks += [_load_heads(kp[p], PAGE_SIZE).astype(BF16) for p in range(nhead - 1, -1, -1)]
    vs = [lambda: _pad_rows(_load_heads(vn_ref, t_new), PAGE_SIZE).astype(BF16)]
    vs += [functools.partial(lambda p: _load_heads(vp[p], PAGE_SIZE).astype(BF16), p)
           for p in range(nhead - 1, -1, -1)]
    acc, carry = _sb_sample_blocks(qb, ks, vs, _suffix_sum_matrix(PAGE_SIZE),
                                   jnp.zeros((rows, PAGE_SIZE), F32), col < _row_time(t_new, PAGE_SIZE))
    y_ref[...] = _unstack_heads(acc, t_new)
    acc_ref[...] = acc
    car_ref[...] = carry
    top = jnp.max(carry, axis=0, keepdims=True)
    live_ref[...] = jnp.broadcast_to(top > SB_DEAD_LOG, live_ref.shape).astype(jnp.int32)


def _sb_sample_tail_body(pt_ref, live_ref, q_ref, acc_ref, car_ref, yin_ref, *refs, ntail, t_new):
    kp, vp, o_ref = refs[:ntail], refs[ntail:2 * ntail], refs[-1]
    b = pl.program_id(0)

    @pl.when(live_ref[b] == 0)
    def _done():
        o_ref[...] = yin_ref[...]

    @pl.when(live_ref[b] != 0)
    def _more():
        qb = _stack_heads(q_ref[...], t_new).astype(BF16)
        ks = [_load_heads(kp[p], PAGE_SIZE).astype(BF16) for p in range(ntail - 1, -1, -1)]
        vs = [functools.partial(lambda p: _load_heads(vp[p], PAGE_SIZE).astype(BF16), p)
              for p in range(ntail - 1, -1, -1)]
        acc, _ = _sb_sample_blocks(qb, ks, vs, _suffix_sum_matrix(PAGE_SIZE), car_ref[...], None)
        o_ref[...] = _unstack_heads(acc_ref[...] + acc, t_new)


PAGE_BLOCK = (None, None, PAGE_SIZE * N_HEADS, D_HEAD)


def _paged_specs(layer, npages, pages):
    def spec(p):
        return pl.BlockSpec(PAGE_BLOCK, lambda b, pt: (layer, pt[b * npages + p], 0, 0))
    return [spec(p) for p in pages]


def _paged_sample_body(pt_ref, qsb_ref, ksbn_ref, vsbn_ref, qmb_ref, kmbn_ref, vmbn_ref, *refs,
                       nhead, npages, t_new):
    sbk, sbv = refs[:nhead], refs[nhead:2 * nhead]
    mbk, mbv = refs[2 * nhead:2 * nhead + npages], refs[2 * nhead + npages:2 * nhead + 2 * npages]
    ysb_ref, ymb_ref, acc_ref, car_ref, live_ref = refs[-5:]
    _sb_sample_head(qsb_ref, ksbn_ref, vsbn_ref, sbk, sbv, ysb_ref, acc_ref, car_ref, live_ref, t_new)
    _moba_sample(qmb_ref, kmbn_ref, vmbn_ref, mbk, mbv, ymb_ref, t_new)


def _paged_sample(layer, page_table, q_sb, sb_new_k, sb_new_v, q_mb, mb_new_k, mb_new_v,
                  cache_sb_k, cache_sb_v, cache_mb_k, cache_mb_v, y_sb, y_mb, rows_p, nseq, t_new):
    npages = page_table.shape[1]
    nhead = min(SB_HEAD_PAGES, npages - 1)
    ntail = npages - nhead
    rows = N_HEADS * t_new
    blk0 = rows_p // t_new
    pt_flat = page_table.reshape(-1)
    q_spec = pl.BlockSpec((t_new, GROUP_W), lambda b, pt: (blk0 + b, 0))
    new_spec = pl.BlockSpec((None, t_new * N_HEADS, D_HEAD), lambda b, pt: (layer, b, 0))
    head_pages = list(range(ntail, npages))
    n_in = 7 + 2 * nhead + 2 * npages
    y_sb, y_mb, acc, carry, live = pl.pallas_call(
        functools.partial(_paged_sample_body, nhead=nhead, npages=npages, t_new=t_new),
        grid_spec=pltpu.PrefetchScalarGridSpec(
            num_scalar_prefetch=1,
            grid=(nseq,),
            in_specs=[q_spec, new_spec, new_spec, q_spec, new_spec, new_spec]
            + 2 * _paged_specs(layer, npages, head_pages) + 2 * _paged_specs(layer, npages, range(npages))
            + [pl.BlockSpec(memory_space=pl.ANY)] * 2,
            out_specs=[q_spec, q_spec,
                       pl.BlockSpec((rows, GROUP_W), lambda b, pt: (b, 0)),
                       pl.BlockSpec((rows, PAGE_SIZE), lambda b, pt: (b, 0)),
                       pl.BlockSpec((None, 8, PAGE_SIZE), lambda b, pt: (b, 0, 0))],
        ),
        out_shape=[jax.ShapeDtypeStruct(y_sb.shape, F32), jax.ShapeDtypeStruct(y_mb.shape, F32),
                   jax.ShapeDtypeStruct((nseq * rows, GROUP_W), F32),
                   jax.ShapeDtypeStruct((nseq * rows, PAGE_SIZE), F32),
                   jax.ShapeDtypeStruct((nseq, 8, PAGE_SIZE), jnp.int32)],
        input_output_aliases={n_in: 0, n_in + 1: 1},
        compiler_params=_params("arbitrary"),
        name="paged_sample",
    )(pt_flat, q_sb, sb_new_k, sb_new_v, q_mb, mb_new_k, mb_new_v,
      *([cache_sb_k] * nhead), *([cache_sb_v] * nhead), *([cache_mb_k] * npages), *([cache_mb_v] * npages),
      y_sb, y_mb)
    live = live[:, 0, 0]

    def tail_spec(p):
        return pl.BlockSpec(PAGE_BLOCK, lambda b, pt, lv: (
            layer, jnp.where(lv[b] != 0, pt[b * npages + p], 0), 0, 0))

    tail_specs = [tail_spec(p) for p in range(ntail)]
    row_spec = pl.BlockSpec((t_new, GROUP_W), lambda b, pt, lv: (blk0 + b, 0))

    def tail(y):
        return pl.pallas_call(
            functools.partial(_sb_sample_tail_body, ntail=ntail, t_new=t_new),
            grid_spec=pltpu.PrefetchScalarGridSpec(
                num_scalar_prefetch=2,
                grid=(nseq,),
                in_specs=[row_spec,
                          pl.BlockSpec((rows, GROUP_W), lambda b, pt, lv: (b, 0)),
                          pl.BlockSpec((rows, PAGE_SIZE), lambda b, pt, lv: (b, 0)), row_spec]
                + tail_specs + tail_specs,
                out_specs=row_spec,
            ),
            out_shape=jax.ShapeDtypeStruct(y.shape, F32),
            input_output_aliases={5: 0},
            compiler_params=_params("arbitrary"),
            name="sb_sample_tail",
        )(pt_flat, live, q_sb, acc, carry, y, *([cache_sb_k] * ntail), *([cache_sb_v] * ntail))

    y_sb = lax.cond(jnp.any(live != 0), tail, lambda y: y, y_sb)
    return y_sb, y_mb


def _moba_prompt_body(q_ref, k_ref, vt_ref, km_ref, o_ref, *, tq, nb, hp):
    qi = pl.program_id(2)
    nbp = -(-nb // 8) * 8
    blk = lax.broadcasted_iota(jnp.int32, (nbp, tq), 0)
    lane = lax.broadcasted_iota(jnp.int32, (tq, D_HEAD), 1)
    qbs, q_augs = [], []
    for h in range(hp):
        q = _head_cols(q_ref, h)
        q_hi, q_lo = _split_hi_lo(q)
        km_hi, km_lo = _split_hi_lo(_pad_rows(_head_cols(km_ref, h), nbp))
        gate = _dot_nt(km_hi, q_hi) + _dot_nt(km_hi, q_lo) + _dot_nt(km_lo, q_hi)
        gate = jnp.where(blk < qi, gate, NEG_INF)
        rank = jnp.zeros(gate.shape, jnp.int32)
        for m in range(nb):
            gm = gate[m:m + 1, :]
            beats = (gm > gate) | ((gm == gate) & (m < blk))
            rank = rank + beats.astype(jnp.int32)
        bias_t = jnp.where((blk < qi) & (rank < MOBA_TOPK), 0.0, NEG_INF)
        bias = _pad_rows(bias_t, D_HEAD).T
        qbs.append(q.astype(BF16))
        q_augs.append(jnp.concatenate([qbs[h], bias.astype(BF16)], axis=1))

    key = lax.broadcasted_iota(jnp.int32, (tq, tq), 0)
    qry = lax.broadcasted_iota(jnp.int32, (tq, tq), 1)

    def kv(n, h):
        off = pl.multiple_of(n * tq, tq)
        rows = pl.ds(h * D_HEAD, D_HEAD)
        return k_ref[pl.ds(off, tq), rows], vt_ref[rows, pl.ds(off, tq)]

    kvs = [kv(qi, h) for h in range(hp)]
    ss = [jnp.where(key <= qry, _dot_nt(kvs[h][0], qbs[h]) * ATT_SCALE, NEG_INF) for h in range(hp)]
    ms = [jnp.max(s, axis=0, keepdims=True) for s in ss]
    ps = [jnp.exp(s - m) for s, m in zip(ss, ms)]
    ls = [jnp.sum(p, axis=0, keepdims=True) for p in ps]
    accs = [_dot(kvs[h][1], p.astype(BF16)) for h, p in enumerate(ps)]

    def past(n, carry):
        ms, ls, accs = carry
        one_hot = jnp.where(lane == n, 1.0, 0.0).astype(BF16)
        kvs = [kv(n, h) for h in range(hp)]
        ss = [_dot_nt(jnp.concatenate([kvs[h][0], one_hot], axis=1), q_augs[h]) * ATT_SCALE for h in range(hp)]
        m_new = [jnp.maximum(m, jnp.max(s, axis=0, keepdims=True)) for m, s in zip(ms, ss)]
        alphas = [jnp.exp(m - mn) for m, mn in zip(ms, m_new)]
        ps = [jnp.exp(s - mn) for s, mn in zip(ss, m_new)]
        ls = [a * l + jnp.sum(p, axis=0, keepdims=True) for a, l, p in zip(alphas, ls, ps)]
        accs = [a * acc + _dot(kv_[1], p.astype(BF16)) for a, acc, p, kv_ in zip(alphas, accs, ps, kvs)]
        return m_new, ls, accs

    _, ls, accs = lax.fori_loop(0, qi, past, (ms, ls, accs))
    o_ref[...] = jnp.concatenate([(acc / l).T for acc, l in zip(accs, ls)], axis=1)


def _moba_prompt(q_all, k_bf, vt_bf, kmean, batch, seq, hp):
    tq = MOBA_BLOCK
    nq = seq // tq
    wide = hp * D_HEAD
    return pl.pallas_call(
        functools.partial(_moba_prompt_body, tq=tq, nb=nq, hp=hp),
        grid=(batch, N_HEADS // hp, nq),
        in_specs=[pl.BlockSpec((tq, wide), lambda b, h, i: (b * nq + i, h)),
                  pl.BlockSpec((seq, wide), lambda b, h, i: (b, h)),
                  pl.BlockSpec((wide, seq), lambda b, h, i: (h, b)),
                  pl.BlockSpec((nq, wide), lambda b, h, i: (b, h))],
        out_specs=pl.BlockSpec((tq, wide), lambda b, h, i: (b * nq + i, h)),
        out_shape=jax.ShapeDtypeStruct(q_all.shape, F32),
        compiler_params=_params("arbitrary", "arbitrary", "arbitrary"),
        name="moba_prompt",
    )(q_all, k_bf, vt_bf, kmean)


def _moba_sample(q_ref, kn_ref, vn_ref, kp, vp, o_ref, t_new):
    npages = len(kp)
    rows = N_HEADS * t_new
    ppb = MOBA_BLOCK // PAGE_SIZE
    nblk = npages // ppb
    q_st = _stack_heads(q_ref[...], t_new)
    qb = q_st.astype(BF16)

    k_pages = [_load_heads(kp[p], PAGE_SIZE) for p in range(npages)]
    gates = []
    for n in range(nblk):
        ksum = k_pages[n * ppb].sum(axis=0, keepdims=True)
        for j in range(1, ppb):
            ksum = ksum + k_pages[n * ppb + j].sum(axis=0, keepdims=True)
        gates.append(jnp.sum(q_st * (ksum * (1.0 / MOBA_BLOCK)), axis=1, keepdims=True))
    chosen = []
    for n in range(nblk):
        rank = jnp.zeros((rows, 1), jnp.int32)
        for m in range(nblk):
            if m != n:
                beats = (gates[m] > gates[n]) | ((gates[m] == gates[n]) & (m < n))
                rank = rank + beats.astype(jnp.int32)
        chosen.append(rank < MOBA_TOPK)

    col = lax.broadcasted_iota(jnp.int32, (rows, PAGE_SIZE), 1)
    kn = _pad_rows(_load_heads(kn_ref, t_new), PAGE_SIZE).astype(BF16)
    s_own = jnp.where(col <= _row_time(t_new, PAGE_SIZE), _dot_nt(qb, kn) * ATT_SCALE, NEG_INF)
    s_pages = []
    for p in range(npages):
        s = _dot_nt(qb, k_pages[p].astype(BF16)) * ATT_SCALE
        s_pages.append(jnp.where(chosen[p // ppb], s, NEG_INF))
    m = jnp.max(s_own, axis=1, keepdims=True)
    for s in s_pages:
        m = jnp.maximum(m, jnp.max(s, axis=1, keepdims=True))
    p_own = jnp.exp(s_own - m)
    l = jnp.sum(p_own, axis=1, keepdims=True)
    acc = _dot(p_own.astype(BF16), _pad_rows(_load_heads(vn_ref, t_new), PAGE_SIZE).astype(BF16))
    for p in range(npages):
        pr = jnp.exp(s_pages[p] - m)
        l = l + jnp.sum(pr, axis=1, keepdims=True)
        acc = acc + _dot(pr.astype(BF16), _load_heads(vp[p], PAGE_SIZE).astype(BF16))
    o_ref[...] = _unstack_heads(acc / l, t_new)


def _softmax_pv_streams(ss, vs):
    ms = [jnp.max(s, axis=1, keepdims=True) for s in ss]
    ps = [jnp.exp(s - m) for s, m in zip(ss, ms)]
    ls = [jnp.sum(p, axis=1, keepdims=True) for p in ps]
    return [_dot(p.astype(BF16), v()) / l for p, v, l in zip(ps, vs, ls)]


def _mem_prompt_body(q_ref, mk_ref, mv_ref, o_ref):
    ss = [_dot_nt(_head_cols(q_ref, h).astype(BF16), _head_cols(mk_ref, h).astype(BF16)) * ATT_SCALE
          for h in range(N_HEADS)]
    vs = [functools.partial(lambda h: _head_cols(mv_ref, h).astype(BF16), h) for h in range(N_HEADS)]
    o_ref[...] = jnp.concatenate(_softmax_pv_streams(ss, vs), axis=1)


def _mem_prompt(q_all, mk, mv, batch, seq, tq):
    nq = seq // tq
    n_mem = mk.shape[0] // batch
    kv_spec = pl.BlockSpec((n_mem, GROUP_W), lambda b, i: (b, 0))
    return pl.pallas_call(
        _mem_prompt_body,
        grid=(batch, nq),
        in_specs=[pl.BlockSpec((tq, GROUP_W), lambda b, i: (b * nq + i, 0)), kv_spec, kv_spec],
        out_specs=pl.BlockSpec((tq, GROUP_W), lambda b, i: (b * nq + i, 0)),
        out_shape=jax.ShapeDtypeStruct(q_all.shape, F32),
        compiler_params=_params("arbitrary", "arbitrary"),
        name="mem_prompt",
    )(q_all, mk, mv)


def _mem_sample_body(q_ref, mk_ref, mv_ref, y_hbm, o_ref, *, nb, t_new, n_mem):
    del y_hbm
    qbs = [_stack_heads(q_ref[i * t_new:(i + 1) * t_new, :], t_new).astype(BF16) for i in range(nb)]
    ss = [_dot_nt(qbs[i], _load_heads(mk_ref, n_mem, (i,)).astype(BF16)) * ATT_SCALE for i in range(nb)]
    vs = [functools.partial(lambda i: _load_heads(mv_ref, n_mem, (i,)).astype(BF16), i) for i in range(nb)]
    for i, pv in enumerate(_softmax_pv_streams(ss, vs)):
        o_ref[i * t_new:(i + 1) * t_new, :] = _unstack_heads(pv, t_new)


def _mem_sample(layer, q_all, mem_k, mem_v, y_all, rows_p, nseq, t_new, nb):
    n_mem = mem_k.shape[2] // N_HEADS
    blk0 = rows_p // (nb * t_new)
    kv_spec = pl.BlockSpec((None, nb, n_mem * N_HEADS, D_HEAD), lambda i: (layer, i, 0, 0))
    return pl.pallas_call(
        functools.partial(_mem_sample_body, nb=nb, t_new=t_new, n_mem=n_mem),
        grid=(nseq // nb,),
        in_specs=[pl.BlockSpec((nb * t_new, GROUP_W), lambda i: (blk0 + i, 0)), kv_spec, kv_spec,
                  pl.BlockSpec(memory_space=pl.ANY)],
        out_specs=pl.BlockSpec((nb * t_new, GROUP_W), lambda i: (blk0 + i, 0)),
        out_shape=jax.ShapeDtypeStruct(y_all.shape, F32),
        input_output_aliases={3: 0},
        compiler_params=_params("arbitrary"),
        name="mem_sample",
    )(q_all, mem_k, mem_v, y_all)


def _finish_attn_body(y0, y1, y2, y3, *refs, alpha, nbp):
    x_refs, (w_ref, g_ref, b_ref, o_ref) = refs[:-4], refs[-4:]
    acc = alpha * _read_rows(x_refs, pl.program_id(0), nbp)
    for gi, y_ref in enumerate((y0, y1, y2, y3)):
        acc = acc + _dot(y_ref[...].astype(BF16), w_ref[gi * GROUP_W:(gi + 1) * GROUP_W, :])
    o_ref[...] = _layer_norm(acc, g_ref[...], b_ref[...])


def _finish_attn(layer, parts, x_parts, w_bf16, g, b, alpha, tm, rows_p):
    rows, d_model = parts[0].shape[0], x_parts[0].shape[1]
    nbp = rows_p // tm
    part_spec = pl.BlockSpec((tm, GROUP_W), lambda i: (i, 0))
    vec_spec = pl.BlockSpec((None, 1, d_model), lambda i: (layer, 0, 0))
    return pl.pallas_call(
        functools.partial(_finish_attn_body, alpha=alpha, nbp=nbp),
        grid=(rows // tm,),
        in_specs=[part_spec] * 4 + _row_specs(x_parts, tm, nbp)
        + [pl.BlockSpec((None,) + w_bf16.shape[1:], lambda i: (layer, 0, 0)), vec_spec, vec_spec],
        out_specs=pl.BlockSpec((tm, d_model), lambda i: (i, 0)),
        out_shape=jax.ShapeDtypeStruct((rows, d_model), F32),
        compiler_params=_params("arbitrary"),
        name="out_projection_ln",
    )(*parts, *x_parts, w_bf16, g, b)


def _ffn_body(x_ref, wu_ref, wd_ref, g_ref, b_ref, *refs, alpha, nbp):
    o_refs, (xb_ref, acc_ref) = refs[:-2], refs[-2:]
    i = pl.program_id(0)
    j = pl.program_id(1)

    @pl.when(j == 0)
    def _init():
        xb_ref[...] = x_ref[...].astype(BF16)
        acc_ref[...] = jnp.zeros(acc_ref.shape, F32)

    hid = jnp.square(jnp.maximum(_dot(xb_ref[...], wu_ref[...]), 0.0))
    acc_ref[...] += _dot(hid.astype(BF16), wd_ref[...])

    @pl.when(j == pl.num_programs(1) - 1)
    def _finish():
        y = _layer_norm(alpha * x_ref[...] + acc_ref[...], g_ref[...], b_ref[...])
        if len(o_refs) == 1:
            o_refs[0][...] = y
        else:
            def put(ref):
                ref[...] = y
            pl.when(i < nbp)(functools.partial(put, o_refs[0]))
            pl.when(i >= nbp)(functools.partial(put, o_refs[1]))


def _ffn(layer, x, wu_bf16, wd_bf16, g, b, alpha, tm, tf, rows_p, split_out):
    rows, d_model = x.shape
    d_ff = wu_bf16.shape[2]
    nbp = rows_p // tm
    if split_out:
        out_parts = (jax.ShapeDtypeStruct((rows_p, d_model), F32), jax.ShapeDtypeStruct((rows - rows_p, d_model), F32))
    else:
        out_parts = (jax.ShapeDtypeStruct((rows, d_model), F32),)
    vec_spec = pl.BlockSpec((None, 1, d_model), lambda i, j: (layer, 0, 0))
    return pl.pallas_call(
        functools.partial(_ffn_body, alpha=alpha, nbp=nbp),
        grid=(rows // tm, d_ff // tf),
        in_specs=[pl.BlockSpec((tm, d_model), lambda i, j: (i, 0)),
                  pl.BlockSpec((None, d_model, tf), lambda i, j: (layer, 0, j)),
                  pl.BlockSpec((None, tf, d_model), lambda i, j: (layer, j, 0)), vec_spec, vec_spec],
        out_specs=_row_specs(out_parts, tm, nbp),
        out_shape=list(out_parts),
        scratch_shapes=[pltpu.VMEM((tm, d_model), BF16), pltpu.VMEM((tm, d_model), F32)],
        compiler_params=_params("arbitrary", "arbitrary"),
        name="ffn_ln",
    )(x, wu_bf16, wd_bf16, g, b)


def _rope_tables(positions):
    half = D_HEAD // 2
    inv = 1.0 / (ROPE_THETA ** (jnp.arange(half, dtype=F32) / half))
    ang = positions.astype(F32)[:, None] * inv[None, :]
    cos, sin = jnp.cos(ang), jnp.sin(ang)
    return jnp.concatenate([cos, cos], axis=1), jnp.concatenate([-sin, sin], axis=1)


def _largest_tile(n, cap):
    t = min(n, cap)
    while n % t:
        t //= 2
    return t


def kernel(x_prompt, x_sample, cache_sb_k, cache_sb_v, cache_mb_k, cache_mb_v, cache_mem_k, cache_mem_v, state_pool, page_table, mem_prompt, w_in, w_mem_k, w_mem_v, pool_w, pool_scale, w_out, ln1_g, ln1_b, w_up, w_down, ln2_g, ln2_b):
    batch, seq, d_model = x_prompt.shape
    nseq, t_new, _ = x_sample.shape
    depth = w_in.shape[0]
    npages = page_table.shape[1]
    past = npages * PAGE_SIZE
    n_mem = mem_prompt.shape[1]
    rows_p, rows_s = batch * seq, nseq * t_new
    rows = rows_p + rows_s
    alpha = float((2 * depth) ** 0.25)
    assert d_model == 4 * GROUP_W and w_in.shape[2] == N_GROUPS_IN * GROUP_W
    assert seq % MOBA_BLOCK == 0 and past % MOBA_BLOCK == 0 and t_new <= PAGE_SIZE and t_new % 8 == 0
    assert state_pool.shape[2] == POOL_BUF and cache_sb_k.shape[2:] == (PAGE_SIZE, N_HEADS, D_HEAD)

    tm = _largest_tile(np.gcd(rows_p, rows_s), 512)
    n_phys = cache_sb_k.shape[1]
    paged = lambda c: c.reshape(depth, n_phys, PAGE_SIZE * N_HEADS, D_HEAD)
    csk, csv, cmk, cmv = paged(cache_sb_k), paged(cache_sb_v), paged(cache_mb_k), paged(cache_mb_v)
    memk = cache_mem_k.reshape(depth, nseq, n_mem * N_HEADS, D_HEAD)
    memv = cache_mem_v.reshape(depth, nseq, n_mem * N_HEADS, D_HEAD)
    mem2d = mem_prompt.reshape(batch * n_mem, d_model)

    tm_in = _largest_tile(np.gcd(seq, rows_s), MOBA_BLOCK)
    pos = jnp.concatenate([jnp.arange(seq, dtype=jnp.int32),
                           jnp.tile(past + jnp.arange(t_new, dtype=jnp.int32), tm_in // t_new)])
    cos_tab, sin_tab = _rope_tables(pos)

    w_in_b, w_out_b, w_up_b, w_down_b = (w.astype(BF16) for w in (w_in, w_out, w_up, w_down))
    w_mem_k_b, w_mem_v_b, pool_w_b = w_mem_k.astype(BF16), w_mem_v.astype(BF16), pool_w.astype(BF16)
    vec = lambda v: v.reshape(depth, 1, v.shape[1])
    pool_sc, g1, b1, g2, b2 = vec(pool_scale), vec(ln1_g), vec(ln1_b), vec(ln2_g), vec(ln2_b)
    tf = _largest_tile(w_up.shape[2], 1024)

    x_parts = (x_prompt.reshape(rows_p, d_model), x_sample.reshape(rows_s, d_model))
    states = None
    mem_states, pool_p, pool_s = [], [], []
    for l in range(depth):
        (u, q_sb, q_mb, q_mem), (k_sb, v_sb, k_mb, v_mb), kmean, states = _in_projection(
            l, depth, x_parts, w_in_b, cos_tab, sin_tab, tm_in, rows_p, states)
        pk_sb, pv_sb, pk_mb, pv_mb, sk_sb, sv_sb, sk_mb, sv_mb = states
        u_s = u[rows_p:].reshape(nseq, t_new, GROUP_W)
        u_ext = jnp.concatenate([jnp.zeros((nseq, POOL_HALO - POOL_BUF, GROUP_W), F32), state_pool[l], u_s], axis=1)
        y_pool = _pool_prompt(l, u, pool_w_b, pool_sc, batch, seq, _largest_tile(seq, 512))
        y_pool = _pool_sample(l, u_ext, pool_w_b, pool_sc, y_pool, past, t_new, rows_p)
        y_sb = _sb_prompt(q_sb, k_sb, v_sb, batch, seq, 256, SB_HEADS_PER_STEP)
        y_mb = _moba_prompt(q_mb, k_mb, v_mb, kmean, batch, seq, MOBA_HEADS_PER_STEP)
        y_sb, y_mb = _paged_sample(l, page_table, q_sb, sk_sb, sv_sb, q_mb, sk_mb, sv_mb,
                                   csk, csv, cmk, cmv, y_sb, y_mb, rows_p, nseq, t_new)
        mk = _matmul(l, mem2d, w_mem_k_b)
        mv = _matmul(l, mem2d, w_mem_v_b)
        y_mem = _mem_prompt(q_mem, mk, mv, batch, seq, _largest_tile(seq, 512))
        y_mem = _mem_sample(l, q_mem, memk, memv, y_mem, rows_p, nseq, t_new, _largest_tile(nseq, 8))
        x = _finish_attn(l, (y_pool, y_sb, y_mb, y_mem), x_parts, w_out_b, g1, b1, alpha, tm, rows_p)
        x_parts = _ffn(l, x, w_up_b, w_down_b, g2, b2, alpha, tm, tf, rows_p, l == depth - 1)
        mem_states.append((mk.reshape(batch, n_mem, N_HEADS, D_HEAD), mv.reshape(batch, n_mem, N_HEADS, D_HEAD)))
        pool_p.append(jnp.stack([u[(b + 1) * seq - POOL_BUF:(b + 1) * seq] for b in range(batch)]))
        pool_s.append(u_ext[:, u_ext.shape[1] - POOL_BUF:])

    heads_p = lambda a: a.reshape(depth, batch, seq, N_HEADS, D_HEAD)
    heads_s = lambda a: a.reshape(depth, nseq, t_new, N_HEADS, D_HEAD)
    return (x_parts[0].reshape(batch, seq, d_model), x_parts[1].reshape(nseq, t_new, d_model),
            heads_p(pk_sb), heads_p(pv_sb), heads_p(pk_mb), heads_p(pv_mb),
            jnp.stack([m[0] for m in mem_states]), jnp.stack([m[1] for m in mem_states]), jnp.stack(pool_p),
            heads_s(sk_sb), heads_s(sv_sb), heads_s(sk_mb), heads_s(sv_mb), jnp.stack(pool_s))
```
